```python
import jax, jax.numpy as jnp
from jax import lax
import numpy as np

D_MODEL = 4096
BATCH = 32
SEQ = 256
DEPTH = 2
DEC_BATCH = 8
DEC_SEQ = 2048
PAST_LEN = 512

GRID_W = 64
ROPE_BASE = 10000.0
NORM_EPS = 1e-6
Q_BLOCK = 128

MLA_HEADS = 8
MLA_Q_LORA = 896
MLA_KV_LORA = 512
MLA_NOPE = 128
MLA_ROPE = 64
MLA_V = 128
MLA_WIDTH = MLA_HEADS * MLA_V

GQA_HEADS = 16
GQA_KV_HEADS = 4
GQA_HEAD_DIM = 128
GQA_WIDTH = GQA_HEADS * GQA_HEAD_DIM

MLSTM_HEADS = 8
MLSTM_DK = 128
MLSTM_DV = 128
MLSTM_CHUNK = 64
MLSTM_WIDTH = MLSTM_HEADS * MLSTM_DV

N_BRANCHES = 3
MIX_WIDTH = MLA_WIDTH + GQA_WIDTH + MLSTM_WIDTH

IN_SIZES = (MLA_Q_LORA, MLA_KV_LORA, MLA_ROPE,
            GQA_HEADS * GQA_HEAD_DIM, GQA_KV_HEADS * GQA_HEAD_DIM, GQA_KV_HEADS * GQA_HEAD_DIM,
            MLSTM_HEADS * MLSTM_DK, MLSTM_HEADS * MLSTM_DK, MLSTM_HEADS * MLSTM_DV, MLSTM_WIDTH,
            4 * MLSTM_HEADS, N_BRANCHES * D_MODEL)
IN_COLS = (MLA_Q_LORA + MLA_KV_LORA + MLA_ROPE + GQA_HEADS * GQA_HEAD_DIM + 2 * GQA_KV_HEADS * GQA_HEAD_DIM
           + 2 * MLSTM_HEADS * MLSTM_DK + MLSTM_HEADS * MLSTM_DV + MLSTM_WIDTH + 4 * MLSTM_HEADS + N_BRANCHES * D_MODEL)

N_EXPERTS = 32
TOP_K = 4
D_FF_EXPERT = 2048
SWIGLU_LIMIT = 7.0
SWIGLU_ALPHA = 1.702
MOE_BLOCK = 128

kernel_name = 'hybrid_dit_mla_gqa_mlstm_moe_step'


def rmsnorm(x, g):
    xf = x.astype(jnp.float32)
    y = xf * lax.rsqrt(jnp.mean(xf * xf, axis=-1, keepdims=True) + NORM_EPS)
    return (y * g.astype(jnp.float32)).astype(x.dtype)


def in_offsets():
    return [int(o) for o in np.cumsum(IN_SIZES)[:-1]]


def axial_rope_tables(grid_rows, rot_dim):
    quarter = rot_dim // 4
    row = jnp.repeat(jnp.arange(grid_rows, dtype=jnp.float32), GRID_W)
    col = jnp.tile(jnp.arange(GRID_W, dtype=jnp.float32), grid_rows)
    inv_freq = ROPE_BASE ** (-jnp.arange(quarter, dtype=jnp.float32) / quarter)
    ang_r = row[:, None] * inv_freq
    ang_c = col[:, None] * inv_freq
    ang = jnp.concatenate([ang_r, ang_r, ang_c, ang_c], axis=-1)
    return jnp.cos(ang), jnp.sin(ang)


def apply_axial_rope(x, cos, sin):
    xf = x.astype(jnp.float32)
    a, b, cc, d = jnp.split(xf, 4, axis=-1)
    rot = jnp.concatenate([-b, a, -d, cc], axis=-1)
    return (xf * cos + rot * sin).astype(x.dtype)


def attend(q, k, v):
    B, Tq, Hq, dq = q.shape
    Hkv, dv = k.shape[2], v.shape[-1]
    G = Hq // Hkv
    scale = dq ** -0.5
    kf = k.astype(jnp.float32)
    vf = v.astype(jnp.float32)
    qb = jnp.moveaxis(q.astype(jnp.float32).reshape(B, Tq // Q_BLOCK, Q_BLOCK, Hkv, G, dq), 1, 0)

    def block(qi):
        s = jnp.einsum('bqhgd,bkhd->bhgqk', qi, kf) * scale
        p = jax.nn.softmax(s, axis=-1)
        return jnp.einsum('bhgqk,bkhd->bqhgd', p, vf)

    out = lax.map(block, qb)
    return jnp.moveaxis(out, 0, 1).reshape(B, Tq, Hq, dv).astype(q.dtype)


def mlstm_chunked(q, k, v, i_pre, log_f, C0, n0, m0):
    B, H, T, dk = q.shape
    dv = v.shape[-1]
    L = MLSTM_CHUNK
    nc = T // L

    def chunks(a):
        return jnp.moveaxis(a.reshape(a.shape[:2] + (nc, L) + a.shape[3:]), 2, 0)

    tri = jnp.tril(jnp.ones((L, L), dtype=bool))

    def step(carry, xs):
        C, n, m = carry
        qc, kc, vc, ic, fc = xs
        b = jnp.cumsum(fc, axis=-1)
        d_intra = jnp.where(tri, b[..., :, None] - b[..., None, :] + ic[..., None, :], -jnp.inf)
        d_inter = b + m[..., None]
        m_row = jnp.maximum(d_inter, jnp.max(d_intra, axis=-1))
        w_intra = jnp.exp(d_intra - m_row[..., None])
        w_inter = jnp.exp(d_inter - m_row)
        s = jnp.einsum('bhjd,bhsd->bhjs', qc, kc) * w_intra
        num = jnp.einsum('bhjs,bhsv->bhjv', s, vc) + w_inter[..., None] * jnp.einsum('bhjd,bhdv->bhjv', qc, C)
        den = jnp.sum(s, axis=-1) + w_inter * jnp.einsum('bhjd,bhd->bhj', qc, n)
        h = num / jnp.maximum(jnp.abs(den), jnp.exp(-m_row))[..., None]
        m_new = m_row[..., -1]
        w_state = jnp.exp(b[..., -1:] - b + ic - m_new[..., None])
        decay = jnp.exp(b[..., -1] + m - m_new)
        C_new = decay[..., None, None] * C + jnp.einsum('bhs,bhsd,bhsv->bhdv', w_state, kc, vc)
        n_new = decay[..., None] * n + jnp.einsum('bhs,bhsd->bhd', w_state, kc)
        return (C_new, n_new, m_new), h

    carry, h = lax.scan(step, (C0, n0, m0), (chunks(q), chunks(k), chunks(v), chunks(i_pre), chunks(log_f)))
    return jnp.moveaxis(h, 0, 2).reshape(B, H, T, dv), carry


def mlstm_bidir(q, k, v, gate_pre, C0, n0, m0):
    f32 = jnp.float32
    qh = jnp.swapaxes(q, 1, 2).astype(f32) * (MLSTM_DK ** -0.5)
    kh = jnp.swapaxes(k, 1, 2).astype(f32)
    vh = jnp.swapaxes(v, 1, 2).astype(f32)
    g = jnp.moveaxis(gate_pre.astype(f32), 1, 3)
    C0, n0, m0 = C0.astype(f32), n0.astype(f32), m0.astype(f32)

    def flip(a):
        return jnp.flip(a, axis=2)

    h_f, (C_f, n_f, m_f) = mlstm_chunked(qh, kh, vh, g[:, 0], jax.nn.log_sigmoid(g[:, 1]),
                                          C0[:, 0], n0[:, 0], m0[:, 0])
    h_b, (C_b, n_b, m_b) = mlstm_chunked(flip(qh), flip(kh), flip(vh), flip(g[:, 2]),
                                          flip(jax.nn.log_sigmoid(g[:, 3])), C0[:, 1], n0[:, 1], m0[:, 1])
    h = jnp.swapaxes(h_f + flip(h_b), 1, 2)
    return h, (jnp.stack([C_f, C_b], axis=1), jnp.stack([n_f, n_b], axis=1), jnp.stack([m_f, m_b], axis=1))


def token_mixers(h, p, rope=None, ctx=None):
    B, T, _ = h.shape
    proj = h @ p['w_in']
    (q_a, kv_a, k_rope, gq, gk, gv, mq, mk, mv, mo, mgate, merge) = jnp.split(proj, in_offsets(), axis=-1)

    q_mla = (rmsnorm(q_a, p['mla_q_norm']) @ p['mla_w_qb']).reshape(B, T, MLA_HEADS, MLA_NOPE + MLA_ROPE)
    ckv = rmsnorm(kv_a, p['mla_kv_norm'])
    q_g = rmsnorm(gq.reshape(B, T, GQA_HEADS, GQA_HEAD_DIM), p['gqa_q_norm'])
    k_g = rmsnorm(gk.reshape(B, T, GQA_KV_HEADS, GQA_HEAD_DIM), p['gqa_k_norm'])
    v_g = gv.reshape(B, T, GQA_KV_HEADS, GQA_HEAD_DIM)

    if ctx is None:
        q_m, q_gr = q_mla, q_g
        ckv_all, kr_all, kg_all, vg_all = ckv, k_rope, k_g, v_g
        C0 = jnp.zeros((B, 2, MLSTM_HEADS, MLSTM_DK, MLSTM_DV), jnp.float32)
        n0 = jnp.zeros((B, 2, MLSTM_HEADS, MLSTM_DK), jnp.float32)
        m0 = jnp.zeros((B, 2, MLSTM_HEADS), jnp.float32)
    else:
        (cos_a, sin_a), (cos_b, sin_b) = rope
        ctx_ckv, ctx_kr, ctx_kg, ctx_vg, C0, n0, m0 = ctx
        q_m = jnp.concatenate([q_mla[..., :MLA_NOPE],
                               apply_axial_rope(q_mla[..., MLA_NOPE:], cos_a[:, None], sin_a[:, None])], axis=-1)
        k_rope_lat = apply_axial_rope(k_rope, cos_a, sin_a)
        q_gr = apply_axial_rope(q_g, cos_b[:, None], sin_b[:, None])
        k_gr = apply_axial_rope(k_g, cos_b[:, None], sin_b[:, None])
        ckv_all = jnp.concatenate([ckv, ctx_ckv.astype(ckv.dtype)], axis=1)
        kr_all = jnp.concatenate([k_rope_lat, ctx_kr.astype(k_rope.dtype)], axis=1)
        kg_all = jnp.concatenate([k_gr, ctx_kg.astype(k_g.dtype)], axis=1)
        vg_all = jnp.concatenate([v_g, ctx_vg.astype(v_g.dtype)], axis=1)

    Tk = ckv_all.shape[1]
    kv = (ckv_all @ p['mla_w_kvb']).reshape(B, Tk, MLA_HEADS, MLA_NOPE + MLA_V)
    k_mla = jnp.concatenate([kv[..., :MLA_NOPE],
                             jnp.broadcast_to(kr_all[:, :, None, :], (B, Tk, MLA_HEADS, MLA_ROPE))], axis=-1)
    y_a = attend(q_m, k_mla, kv[..., MLA_NOPE:]).reshape(B, T, MLA_WIDTH)
    y_b = attend(q_gr, kg_all, vg_all).reshape(B, T, GQA_WIDTH)

    gate_pre = mgate.reshape(B, T, 4, MLSTM_HEADS) + p['mlstm_gate_bias']
    h_c, states = mlstm_bidir(mq.reshape(B, T, MLSTM_HEADS, MLSTM_DK), mk.reshape(B, T, MLSTM_HEADS, MLSTM_DK),
                              mv.reshape(B, T, MLSTM_HEADS, MLSTM_DV), gate_pre, C0, n0, m0)
    y_c = (rmsnorm(h_c.astype(h.dtype), p['mlstm_out_norm'])
           * jax.nn.sigmoid(mo).reshape(B, T, MLSTM_HEADS, MLSTM_DV)).reshape(B, T, MLSTM_WIDTH)

    gates = jax.nn.sigmoid(merge).reshape(B, T, N_BRANCHES, D_MODEL)
    wb = p['w_branch']
    z = (gates[..., 0, :] * (y_a @ wb[:MLA_WIDTH])
         + gates[..., 1, :] * (y_b @ wb[MLA_WIDTH:MLA_WIDTH + GQA_WIDTH])
         + gates[..., 2, :] * (y_c @ wb[MLA_WIDTH + GQA_WIDTH:]))
    out = z @ p['w_out']
    if ctx is None:
        return out, (ckv, k_rope, k_g, v_g, states[0], states[1], states[2])
    return out, None


def moe_ffn(x, p):
    shp = x.shape
    d = shp[-1]
    xt = x.reshape(-1, d)
    n_tok = xt.shape[0]
    logits = (xt @ p['router_w'] + p['router_b']).astype(jnp.float32)
    top_val, top_idx = lax.top_k(logits, TOP_K)
    gate = jax.nn.softmax(top_val, axis=-1)
    n_assign = n_tok * TOP_K
    e_flat = top_idx.reshape(n_assign)
    tok_flat = jnp.repeat(jnp.arange(n_tok, dtype=jnp.int32), TOP_K)
    order = jnp.argsort(e_flat)
    e_sorted = e_flat[order]
    counts = jnp.bincount(e_flat, length=N_EXPERTS)
    starts = jnp.cumsum(counts) - counts
    padded = (counts + MOE_BLOCK - 1) // MOE_BLOCK * MOE_BLOCK
    pad_end = jnp.cumsum(padded)
    pad_start = pad_end - padded
    dest = pad_start[e_sorted] + jnp.arange(n_assign) - starts[e_sorted]
    n_slots = n_assign + N_EXPERTS * MOE_BLOCK
    n_blocks = n_slots // MOE_BLOCK
    slot_tok = jnp.full((n_slots,), n_tok, jnp.int32).at[dest].set(tok_flat[order])
    slot_gate = jnp.zeros((n_slots,), jnp.float32).at[dest].set(gate.reshape(n_assign)[order])
    blk_expert = jnp.minimum(jnp.searchsorted(pad_end, jnp.arange(n_blocks) * MOE_BLOCK, side='right'),
                             N_EXPERTS - 1)
    x_pad = jnp.concatenate([xt, jnp.zeros((1, d), xt.dtype)], axis=0)
    xb = x_pad[slot_tok].reshape(n_blocks, MOE_BLOCK, d)

    def expert_block(args):
        xi, e = args
        g = xi @ p['moe_w_gate'][e] + p['moe_b_gate'][e]
        u = xi @ p['moe_w_up'][e] + p['moe_b_up'][e]
        g = jnp.minimum(g, SWIGLU_LIMIT)
        u = jnp.clip(u, -SWIGLU_LIMIT, SWIGLU_LIMIT)
        return ((u + 1.0) * g * jax.nn.sigmoid(SWIGLU_ALPHA * g)) @ p['moe_w_down'][e] + p['moe_b_down'][e]

    yb = lax.map(expert_block, (xb, blk_expert)).reshape(n_slots, d)
    y = jnp.zeros((n_tok + 1, d), jnp.float32).at[slot_tok].add(yb.astype(jnp.float32) * slot_gate[:, None])
    return y[:n_tok].astype(x.dtype).reshape(shp)


def trunk_layer(x, cond, p, rope=None, ctx=None):
    mod = jax.nn.silu(cond) @ p['w_mod'] + p['b_mod']
    if mod.ndim == 2:
        mod = mod[:, None, :]
    sh1, sc1, g1, sh2, sc2, g2 = jnp.split(mod, 6, axis=-1)
    h = rmsnorm(x, p['norm_attn']) * (1.0 + sc1) + sh1
    mix, ctx_out = token_mixers(h, p, rope, ctx)
    x = x + g1 * mix
    h = rmsnorm(x, p['norm_ffn']) * (1.0 + sc2) + sh2
    x = x + g2 * moe_ffn(h, p)
    return x, ctx_out


def setup_inputs(seed: int = 0) -> dict:
    key = jax.random.key(seed)
    ks = jax.random.split(key, 40)
    f32 = jnp.float32

    def nrm(k, shape, s):
        return jax.random.normal(k, shape, f32) * s

    L = DEPTH
    return {
        'x_prompt': nrm(ks[0], (BATCH, SEQ, D_MODEL), 1.0),
        'x_sample': nrm(ks[1], (DEC_BATCH, DEC_SEQ, D_MODEL), 1.0),
        'cache_mla_ckv': nrm(ks[2], (DEC_BATCH, L, PAST_LEN, MLA_KV_LORA), 1.0),
        'cache_mla_krope': nrm(ks[3], (DEC_BATCH, L, PAST_LEN, MLA_ROPE), 1.0),
        'cache_gqa_k': nrm(ks[4], (DEC_BATCH, L, PAST_LEN, GQA_KV_HEADS, GQA_HEAD_DIM), 1.0),
        'cache_gqa_v': nrm(ks[5], (DEC_BATCH, L, PAST_LEN, GQA_KV_HEADS, GQA_HEAD_DIM), 1.0),
        'state_mlstm_C': nrm(ks[6], (DEC_BATCH, L, 2, MLSTM_HEADS, MLSTM_DK, MLSTM_DV), 0.05),
        'state_mlstm_n': nrm(ks[7], (DEC_BATCH, L, 2, MLSTM_HEADS, MLSTM_DK), 0.1),
        'state_mlstm_m': 1.0 + nrm(ks[8], (DEC_BATCH, L, 2, MLSTM_HEADS), 0.5),
        'c': nrm(ks[9], (DEC_BATCH, D_MODEL), 1.0),
        'c_ctx': nrm(ks[10], (D_MODEL,), 1.0),
        'w_mod': nrm(ks[11], (L, D_MODEL, 6 * D_MODEL), 0.5 * D_MODEL ** -0.5),
        'b_mod': nrm(ks[12], (L, 6 * D_MODEL), 0.01),
        'norm_attn': 1.0 + nrm(ks[13], (L, D_MODEL), 0.02),
        'norm_ffn': 1.0 + nrm(ks[14], (L, D_MODEL), 0.02),
        'w_in': nrm(ks[15], (L, D_MODEL, IN_COLS), D_MODEL ** -0.5),
        'mla_q_norm': 1.0 + nrm(ks[16], (L, MLA_Q_LORA), 0.02),
        'mla_w_qb': nrm(ks[17], (L, MLA_Q_LORA, MLA_HEADS * (MLA_NOPE + MLA_ROPE)), MLA_Q_LORA ** -0.5),
        'mla_kv_norm': 1.0 + nrm(ks[18], (L, MLA_KV_LORA), 0.02),
        'mla_w_kvb': nrm(ks[19], (L, MLA_KV_LORA, MLA_HEADS * (MLA_NOPE + MLA_V)), MLA_KV_LORA ** -0.5),
        'gqa_q_norm': 1.0 + nrm(ks[20], (L, GQA_HEAD_DIM), 0.02),
        'gqa_k_norm': 1.0 + nrm(ks[21], (L, GQA_HEAD_DIM), 0.02),
        'mlstm_gate_bias': jnp.array([0.0, 3.0, 0.0, 3.0], f32)[None, :, None] + nrm(ks[22], (L, 4, MLSTM_HEADS), 0.3),
        'mlstm_out_norm': 1.0 + nrm(ks[23], (L, MLSTM_HEADS, MLSTM_DV), 0.02),
        'w_branch': nrm(ks[24], (L, MIX_WIDTH, D_MODEL), (MIX_WIDTH // N_BRANCHES) ** -0.5),
        'w_out': nrm(ks[25], (L, D_MODEL, D_MODEL), D_MODEL ** -0.5),
        'router_w': nrm(ks[26], (L, D_MODEL, N_EXPERTS), D_MODEL ** -0.5),
        'router_b': nrm(ks[27], (L, N_EXPERTS), 0.01),
        'moe_w_gate': nrm(ks[28], (L, N_EXPERTS, D_MODEL, D_FF_EXPERT), D_MODEL ** -0.5),
        'moe_b_gate': nrm(ks[29], (L, N_EXPERTS, D_FF_EXPERT), 0.01),
        'moe_w_up': nrm(ks[30], (L, N_EXPERTS, D_MODEL, D_FF_EXPERT), D_MODEL ** -0.5),
        'moe_b_up': nrm(ks[31], (L, N_EXPERTS, D_FF_EXPERT), 0.01),
        'moe_w_down': nrm(ks[32], (L, N_EXPERTS, D_FF_EXPERT, D_MODEL), D_FF_EXPERT ** -0.5),
        'moe_b_down': nrm(ks[33], (L, N_EXPERTS, D_MODEL), 0.01),
        'final_norm': 1.0 + nrm(ks[34], (D_MODEL,), 0.02),
    }


def reference(x_prompt, x_sample, cache_mla_ckv, cache_mla_krope, cache_gqa_k, cache_gqa_v,
              state_mlstm_C, state_mlstm_n, state_mlstm_m, c, c_ctx,
              w_mod, b_mod, norm_attn, norm_ffn, w_in, mla_q_norm, mla_w_qb, mla_kv_norm, mla_w_kvb,
              gqa_q_norm, gqa_k_norm, mlstm_gate_bias, mlstm_out_norm, w_branch, w_out,
              router_w, router_b, moe_w_gate, moe_b_gate, moe_w_up, moe_b_up, moe_w_down, moe_b_down,
              final_norm):
    grid_rows = x_sample.shape[1] // GRID_W
    rope = (axial_rope_tables(grid_rows, MLA_ROPE), axial_rope_tables(grid_rows, GQA_HEAD_DIM))
    xc, xs = x_prompt, x_sample
    ctx_layers = []
    for l in range(DEPTH):
        p = {'w_mod': w_mod[l], 'b_mod': b_mod[l], 'norm_attn': norm_attn[l], 'norm_ffn': norm_ffn[l],
             'w_in': w_in[l], 'mla_q_norm': mla_q_norm[l], 'mla_w_qb': mla_w_qb[l],
             'mla_kv_norm': mla_kv_norm[l], 'mla_w_kvb': mla_w_kvb[l],
             'gqa_q_norm': gqa_q_norm[l], 'gqa_k_norm': gqa_k_norm[l],
             'mlstm_gate_bias': mlstm_gate_bias[l], 'mlstm_out_norm': mlstm_out_norm[l],
             'w_branch': w_branch[l], 'w_out': w_out[l], 'router_w': router_w[l], 'router_b': router_b[l],
             'moe_w_gate': moe_w_gate[l], 'moe_b_gate': moe_b_gate[l], 'moe_w_up': moe_w_up[l],
             'moe_b_up': moe_b_up[l], 'moe_w_down': moe_w_down[l], 'moe_b_down': moe_b_down[l]}
        xc, ctx_l = trunk_layer(xc, c_ctx, p)
        ctx_layers.append(ctx_l)
        cache_l = (cache_mla_ckv[:, l], cache_mla_krope[:, l], cache_gqa_k[:, l], cache_gqa_v[:, l],
                   state_mlstm_C[:, l], state_mlstm_n[:, l], state_mlstm_m[:, l])
        xs, _ = trunk_layer(xs, c, p, rope, cache_l)
    y_prompt = rmsnorm(xc, final_norm)
    y_sample = rmsnorm(xs, final_norm)
    new_mla_ckv = jnp.stack([t[0] for t in ctx_layers], axis=1)
    new_mla_krope = jnp.stack([t[1] for t in ctx_layers], axis=1)
    new_gqa_k = jnp.stack([t[2] for t in ctx_layers], axis=1)
    new_gqa_v = jnp.stack([t[3] for t in ctx_layers], axis=1)
    new_mlstm_C = jnp.stack([t[4] for t in ctx_layers], axis=1)
    new_mlstm_n = jnp.stack([t[5] for t in ctx_layers], axis=1)
    new_mlstm_m = jnp.stack([t[6] for t in ctx_layers], axis=1)
    return (y_prompt, y_sample, new_mla_ckv, new_mla_krope, new_gqa_k, new_gqa_v,
            new_mlstm_C, new_mlstm_n, new_mlstm_m)
```

```python
import functools

import jax
import jax.numpy as jnp
import numpy as np
from jax import lax
from jax.experimental import pallas as pl
from jax.experimental.pallas import tpu as pltpu

F32 = jnp.float32
BF16 = jnp.bfloat16
I32 = jnp.int32

GRID_W = 64
ROPE_BASE = 10000.0
NORM_EPS = 1e-6

MLA_HEADS = 8
MLA_Q_LORA = 896
MLA_KV_LORA = 512
MLA_NOPE = 128
MLA_ROPE = 64
MLA_V = 128
GQA_HEADS = 16
GQA_KV_HEADS = 4
GQA_HEAD_DIM = 128
MLSTM_HEADS = 8
MLSTM_DK = 128
MLSTM_DV = 128
MLSTM_CHUNK = 64
N_BRANCHES = 3
TOP_K = 4
SWIGLU_LIMIT = 7.0
SWIGLU_ALPHA = 1.702

LANE = 128
VMEM_LIMIT = 56 * 1024 * 1024
MOE_ROWS = 512
GATHER_ROWS = 512

MLA_WIDTH = MLA_HEADS * MLA_V
GQA_WIDTH = GQA_HEADS * GQA_HEAD_DIM
GQA_KV_WIDTH = GQA_KV_HEADS * GQA_HEAD_DIM
MLSTM_WIDTH = MLSTM_HEADS * MLSTM_DV
Q_LORA_PAD = 1024


def _tile(n, cap):
    for t in (2048, 1024, 512, 256, 128, 64, 32, 16, 8):
        if t <= cap and n % t == 0:
            return t
    raise ValueError(f"no tile for {n}")


def _params(sem, vmem=VMEM_LIMIT):
    return pltpu.CompilerParams(dimension_semantics=sem, vmem_limit_bytes=vmem)


def _proj_layout(d_model):
    segs = [("merge", N_BRANCHES * d_model, N_BRANCHES * d_model), ("gq", GQA_WIDTH, GQA_WIDTH),
            ("mq", MLSTM_WIDTH, MLSTM_WIDTH), ("mk", MLSTM_WIDTH, MLSTM_WIDTH), ("mv", MLSTM_WIDTH, MLSTM_WIDTH),
            ("mo", MLSTM_WIDTH, MLSTM_WIDTH), ("qa", MLA_Q_LORA, Q_LORA_PAD), ("kva", MLA_KV_LORA, MLA_KV_LORA),
            ("gk", GQA_KV_WIDTH, GQA_KV_WIDTH), ("gv", GQA_KV_WIDTH, GQA_KV_WIDTH),
            ("kr", MLA_ROPE, LANE), ("mg", 4 * MLSTM_HEADS, LANE)]
    off, out = 0, {}
    for name, w, wp in segs:
        assert off % wp == 0, (name, off, wp)
        out[name] = (off, w, wp)
        off += wp
    total = -(-off // 512) * 512
    return out, total


def _build_w_in(w_in_l, d_model):
    lay, total = _proj_layout(d_model)
    ref_sizes = dict(qa=MLA_Q_LORA, kva=MLA_KV_LORA, kr=MLA_ROPE, gq=GQA_WIDTH, gk=GQA_KV_WIDTH, gv=GQA_KV_WIDTH,
                     mq=MLSTM_WIDTH, mk=MLSTM_WIDTH, mv=MLSTM_WIDTH, mo=MLSTM_WIDTH, mg=4 * MLSTM_HEADS,
                     merge=N_BRANCHES * d_model)
    ref_order = ["qa", "kva", "kr", "gq", "gk", "gv", "mq", "mk", "mv", "mo", "mg", "merge"]
    src, o = {}, 0
    for name in ref_order:
        src[name] = (o, ref_sizes[name])
        o += ref_sizes[name]
    wb = w_in_l.astype(BF16)
    pieces, used = [], 0
    for name, (off, w, wp) in lay.items():
        s0, sw = src[name]
        pieces.append(wb[:, s0:s0 + sw])
        if wp > w:
            pieces.append(jnp.zeros((wb.shape[0], wp - w), BF16))
        used = off + wp
    if total > used:
        pieces.append(jnp.zeros((wb.shape[0], total - used), BF16))
    return jnp.concatenate(pieces, axis=1)


def _build_w_qb(w_qb_l):
    w = w_qb_l.astype(BF16).reshape(MLA_Q_LORA, MLA_HEADS, MLA_NOPE + MLA_ROPE)
    nope = w[:, :, :MLA_NOPE].reshape(MLA_Q_LORA, MLA_HEADS * MLA_NOPE)
    rope = jnp.pad(w[:, :, MLA_NOPE:], ((0, 0), (0, 0), (0, LANE - MLA_ROPE))).reshape(MLA_Q_LORA, MLA_HEADS * LANE)
    w2 = jnp.concatenate([nope, rope], axis=1)
    return jnp.pad(w2, ((0, Q_LORA_PAD - MLA_Q_LORA), (0, 0)))


def _build_w_kvb(w_kvb_l):
    w = w_kvb_l.astype(BF16).reshape(MLA_KV_LORA, MLA_HEADS, MLA_NOPE + MLA_V)
    return jnp.concatenate([w[:, :, :MLA_NOPE].reshape(MLA_KV_LORA, -1), w[:, :, MLA_NOPE:].reshape(MLA_KV_LORA, -1)],
                           axis=1)


def _rope_tables(n_prompt, dec_batch, dec_seq, rot_dim):
    quarter = rot_dim // 4
    grid_rows = dec_seq // GRID_W
    row = jnp.repeat(jnp.arange(grid_rows, dtype=F32), GRID_W)
    col = jnp.tile(jnp.arange(GRID_W, dtype=F32), grid_rows)
    inv_freq = ROPE_BASE ** (-jnp.arange(quarter, dtype=F32) / quarter)
    ang_r = row[:, None] * inv_freq
    ang_c = col[:, None] * inv_freq
    ang = jnp.concatenate([ang_r, ang_r, ang_c, ang_c], axis=-1)
    cos = jnp.pad(jnp.cos(ang), ((0, 0), (0, LANE - rot_dim)), constant_values=1.0)
    sin = jnp.pad(jnp.sin(ang), ((0, 0), (0, LANE - rot_dim)))
    cos = jnp.concatenate([jnp.ones((n_prompt, LANE), F32), jnp.tile(cos, (dec_batch, 1))], axis=0)
    sin = jnp.concatenate([jnp.zeros((n_prompt, LANE), F32), jnp.tile(sin, (dec_batch, 1))], axis=0)
    return cos, sin


def _mod_kernel(c_ref, w_ref, b_ref, o_ref):
    c = c_ref[...]
    a = (c * jax.nn.sigmoid(c)).astype(BF16)
    o_ref[...] = jnp.dot(a, w_ref[...].astype(BF16), preferred_element_type=F32) + b_ref[...]


def _modulation(cond, w_mod_l, b_mod_l):
    rows, d = cond.shape
    n = w_mod_l.shape[1]
    tn = _tile(n, 512)
    return pl.pallas_call(
        _mod_kernel, grid=(n // tn,),
        in_specs=[pl.BlockSpec((rows, d), lambda j: (0, 0)), pl.BlockSpec((d, tn), lambda j: (0, j)),
                  pl.BlockSpec((1, tn), lambda j: (0, j))],
        out_specs=pl.BlockSpec((rows, tn), lambda j: (0, j)),
        out_shape=jax.ShapeDtypeStruct((rows, n), F32),
        compiler_params=_params(("parallel",)))(cond, w_mod_l, b_mod_l.reshape(1, n))


def _mm_kernel(x_ref, w_ref, o_ref):
    o_ref[...] = jnp.dot(x_ref[...], w_ref[...], preferred_element_type=F32).astype(o_ref.dtype)


def _matmul(x, w, out_dtype, tm_cap, tn_cap):
    m, k = x.shape
    n = w.shape[1]
    tm, tn = _tile(m, tm_cap), _tile(n, tn_cap)
    return pl.pallas_call(
        _mm_kernel, grid=(m // tm, n // tn),
        in_specs=[pl.BlockSpec((tm, k), lambda i, j: (i, 0)), pl.BlockSpec((k, tn), lambda i, j: (0, j))],
        out_specs=pl.BlockSpec((tm, tn), lambda i, j: (i, j)),
        out_shape=jax.ShapeDtypeStruct((m, n), out_dtype),
        compiler_params=_params(("parallel", "parallel")))(x, w)


def _mod_row(i, tm, n_prompt, dec_seq):
    r0 = i * tm
    return jnp.where(r0 < n_prompt, 0, 1 + (r0 - n_prompt) // dec_seq)


def _norm_mod_kernel(x_ref, g_ref, sc_ref, sh_ref, o_ref):
    x = x_ref[...]
    r = lax.rsqrt(jnp.mean(x * x, axis=-1, keepdims=True) + NORM_EPS)
    y = x * r * g_ref[...]
    o_ref[...] = (y * (1.0 + sc_ref[...]) + sh_ref[...]).astype(o_ref.dtype)


def _norm_mod(x, g, mod3, sc_blk, sh_blk, n_prompt, dec_seq):
    n, d = x.shape
    tm = _tile(np.gcd(n_prompt, dec_seq), 256)
    mrow = functools.partial(_mod_row, tm=tm, n_prompt=n_prompt, dec_seq=dec_seq)
    return pl.pallas_call(
        _norm_mod_kernel, grid=(n // tm,),
        in_specs=[pl.BlockSpec((tm, d), lambda i: (i, 0)), pl.BlockSpec((1, d), lambda i: (0, 0)),
                  pl.BlockSpec((None, 1, d), lambda i: (mrow(i), 0, sc_blk)),
                  pl.BlockSpec((None, 1, d), lambda i: (mrow(i), 0, sh_blk))],
        out_specs=pl.BlockSpec((tm, d), lambda i: (i, 0)),
        out_shape=jax.ShapeDtypeStruct((n, d), BF16),
        compiler_params=_params(("parallel",)))(x, g.reshape(1, d), mod3, mod3)


def _rms_kernel(x_ref, g_ref, o_ref):
    x = x_ref[...]
    r = lax.rsqrt(jnp.mean(x * x, axis=-1, keepdims=True) + NORM_EPS)
    o_ref[...] = x * r * g_ref[...]


def _rmsnorm(x, g):
    n, d = x.shape
    tm = _tile(n, 256)
    return pl.pallas_call(
        _rms_kernel, grid=(n // tm,),
        in_specs=[pl.BlockSpec((tm, d), lambda i: (i, 0)), pl.BlockSpec((1, d), lambda i: (0, 0))],
        out_specs=pl.BlockSpec((tm, d), lambda i: (i, 0)),
        out_shape=jax.ShapeDtypeStruct((n, d), F32),
        compiler_params=_params(("parallel",)))(x, g.reshape(1, d))


def _rot_half(x, half):
    lane = lax.broadcasted_iota(I32, x.shape, 1)
    first = (lane % (2 * half)) < half
    return jnp.where(first, -pltpu.roll(x, LANE - half, 1), pltpu.roll(x, half, 1))


def _prep_kernel(qa_ref, kva_ref, gq_ref, gk_ref, gv_ref, kr_ref, ca_ref, sa_ref, cb_ref, sb_ref,
                 qn_ref, kvn_ref, gqn_ref, gkn_ref, wqb_ref,
                 qm_ref, ckv32_ref, ckv16_ref, kr16_ref, qg_ref, kg32_ref, kg16_ref, v16_ref):
    ca, sa, cb, sb = ca_ref[...], sa_ref[...], cb_ref[...], sb_ref[...]
    qa = qa_ref[...]
    r = lax.rsqrt(jnp.sum(qa * qa, axis=-1, keepdims=True) * (1.0 / MLA_Q_LORA) + NORM_EPS)
    qn = (qa * r * qn_ref[...]).astype(BF16)
    q = jnp.dot(qn, wqb_ref[...], preferred_element_type=F32)
    scale_a = (MLA_NOPE + MLA_ROPE) ** -0.5
    nw = MLA_HEADS * MLA_NOPE
    qm_ref[:, :nw] = (q[:, :nw] * scale_a).astype(BF16)
    for h in range(MLA_HEADS):
        x = q[:, nw + h * LANE: nw + (h + 1) * LANE]
        qm_ref[:, nw + h * LANE: nw + (h + 1) * LANE] = (
            (x * ca + _rot_half(x, MLA_ROPE // 4) * sa) * scale_a).astype(BF16)
    kva = kva_ref[...]
    r = lax.rsqrt(jnp.mean(kva * kva, axis=-1, keepdims=True) + NORM_EPS)
    ckv = kva * r * kvn_ref[...]
    ckv32_ref[...] = ckv
    ckv16_ref[...] = ckv.astype(BF16)
    kr = kr_ref[...]
    kr16_ref[...] = (kr * ca + _rot_half(kr, MLA_ROPE // 4) * sa).astype(BF16)
    scale_b = GQA_HEAD_DIM ** -0.5
    gqn, gkn = gqn_ref[...], gkn_ref[...]
    for h in range(GQA_HEADS):
        sl = slice(h * LANE, (h + 1) * LANE)
        x = gq_ref[:, sl]
        x = x * lax.rsqrt(jnp.mean(x * x, axis=-1, keepdims=True) + NORM_EPS) * gqn
        qg_ref[:, sl] = ((x * cb + _rot_half(x, GQA_HEAD_DIM // 4) * sb) * scale_b).astype(BF16)
    for h in range(GQA_KV_HEADS):
        sl = slice(h * LANE, (h + 1) * LANE)
        x = gk_ref[:, sl]
        x = x * lax.rsqrt(jnp.mean(x * x, axis=-1, keepdims=True) + NORM_EPS) * gkn
        kg32_ref[:, sl] = x
        kg16_ref[:, sl] = (x * cb + _rot_half(x, GQA_HEAD_DIM // 4) * sb).astype(BF16)
    v16_ref[...] = gv_ref[...].astype(BF16)


def _prep(proj, lay, tabs, qn, kvn, gqn, gkn, wqb):
    n = proj.shape[0]
    tm = _tile(n, 256)
    ca, sa, cb, sb = tabs

    def pspec(name):
        off, _, wp = lay[name]
        return pl.BlockSpec((tm, wp), lambda i: (i, off // wp))

    def row(w):
        return pl.BlockSpec((tm, w), lambda i: (i, 0))

    def full(a):
        return pl.BlockSpec(a.shape, lambda i: (0,) * a.ndim)

    outs = [(2 * MLA_HEADS * LANE, BF16), (MLA_KV_LORA, F32), (MLA_KV_LORA, BF16), (LANE, BF16),
            (GQA_WIDTH, BF16), (GQA_KV_WIDTH, F32), (GQA_KV_WIDTH, BF16), (GQA_KV_WIDTH, BF16)]
    return pl.pallas_call(
        _prep_kernel, grid=(n // tm,),
        in_specs=[pspec("qa"), pspec("kva"), pspec("gq"), pspec("gk"), pspec("gv"), pspec("kr"),
                  row(LANE), row(LANE), row(LANE), row(LANE), full(qn), full(kvn), full(gqn), full(gkn), full(wqb)],
        out_specs=[row(w) for w, _ in outs],
        out_shape=[jax.ShapeDtypeStruct((n, w), dt) for w, dt in outs],
        compiler_params=_params(("parallel",)))(proj, proj, proj, proj, proj, proj, ca, sa, cb, sb,
                                                 qn, kvn, gqn, gkn, wqb)


_NT = (((1,), (1,)), ((), ()))


def _attn_kernel(*refs, two_part):
    if two_part:
        q1_ref, q2_ref, k1_ref, k2_ref, v_ref, o_ref = refs
    else:
        q1_ref, k1_ref, v_ref, o_ref = refs
    s = lax.dot_general(q1_ref[...], k1_ref[...], _NT, preferred_element_type=F32)
    if two_part:
        s = s + lax.dot_general(q2_ref[...], k2_ref[...], _NT, preferred_element_type=F32)
    m = jnp.max(s, axis=-1, keepdims=True)
    p = jnp.exp(s - m)
    l = jnp.sum(p, axis=-1, keepdims=True)
    o = jnp.dot(p.astype(BF16), v_ref[...], preferred_element_type=F32)
    o_ref[...] = (o / l).astype(o_ref.dtype)


def _attention(q, k, v, out_width, *, batch, tq_len, tk_len, heads, group, q_row0, k_row0, q_col0, k_col0, v_col0,
               q2_col0=None, k2=None, k2_row0=0):
    tq = _tile(tq_len, 256)
    nq = tq_len // tq
    assert q_row0 % tq == 0 and k_row0 % tk_len == 0 and k2_row0 % tk_len == 0
    qb0, kb0, k2b0 = q_row0 // tq, k_row0 // tk_len, k2_row0 // tk_len
    two = k2 is not None
    qspec = lambda c0: pl.BlockSpec((tq, LANE), lambda b, h, i: (qb0 + b * nq + i, c0 + h))
    kspec = lambda c0: pl.BlockSpec((tk_len, LANE), lambda b, h, i: (kb0 + b, c0 + h // group))
    in_specs = [qspec(q_col0)]
    args = [q]
    if two:
        in_specs.append(qspec(q2_col0))
        args.append(q)
    in_specs.append(kspec(k_col0))
    args.append(k)
    if two:
        in_specs.append(pl.BlockSpec((tk_len, LANE), lambda b, h, i: (k2b0 + b, 0)))
        args.append(k2)
    in_specs.append(kspec(v_col0))
    args.append(v)
    return in_specs, args, tq, nq, qb0


def _attention_call(out_buf, q, k, v, **kw):
    batch, heads = kw["batch"], kw["heads"]
    in_specs, args, tq, nq, qb0 = _attention(q, k, v, None, **kw)
    two = kw.get("k2") is not None
    in_specs.append(pl.BlockSpec(memory_space=pl.ANY))
    args.append(out_buf)

    def body(*refs):
        _attn_kernel(*refs[:len(args) - 1], refs[len(args)], two_part=two)

    return pl.pallas_call(
        body, grid=(batch, heads, nq), in_specs=in_specs,
        out_specs=pl.BlockSpec((tq, LANE), lambda b, h, i: (qb0 + b * nq + i, h)),
        out_shape=jax.ShapeDtypeStruct(out_buf.shape, out_buf.dtype),
        input_output_aliases={len(args) - 1: 0},
        compiler_params=_params(("parallel", "parallel", "arbitrary")))(*args)


def _log_sigmoid(x):
    return jnp.minimum(x, 0.0) - jnp.log(1.0 + jnp.exp(-jnp.abs(x)))


def _mlstm_kernel(bias_ref, q_ref, k_ref, v_ref, og_ref, gr_ref, gc_ref, c0_ref, n0_ref, m0_ref, g_ref,
                  y_ref, co_ref, no_ref, mo_ref, hf_ref, hb_ref, c_ref, n_ref, m_ref, *, nc):
    head = pl.program_id(1)
    L = MLSTM_CHUNK
    c_ref[...] = c0_ref[...]
    n_ref[...] = n0_ref[...]
    m_ref[...] = m0_ref[...]
    ri = lax.broadcasted_iota(I32, (L, L), 0)
    ci = lax.broadcasted_iota(I32, (L, L), 1)

    def chunk(d, c):
        rows = pl.ds(pl.multiple_of(c * L, L), L)
        qf = q_ref[rows, :] * (MLSTM_DK ** -0.5)
        kf = k_ref[rows, :]
        qb, kb, vb = qf.astype(BF16), kf.astype(BF16), v_ref[rows, :].astype(BF16)
        gr = gr_ref[c]
        gc = gc_ref[c]
        bi = bias_ref[2 * d * MLSTM_HEADS + head]
        bf = bias_ref[(2 * d + 1) * MLSTM_HEADS + head]
        i_row = gr[2 * d:2 * d + 1, :] + bi
        f_row = _log_sigmoid(gr[2 * d + 1:2 * d + 2, :] + bf)
        i_col = gc[:, 2 * d:2 * d + 1] + bi
        f_col = _log_sigmoid(gc[:, 2 * d + 1:2 * d + 2] + bf)
        mask = (ci <= ri) if d == 0 else (ci >= ri)
        mask_t = (ri <= ci) if d == 0 else (ri >= ci)
        b_col = jnp.sum(jnp.where(mask, f_row, 0.0), axis=1, keepdims=True)
        b_row = jnp.sum(jnp.where(mask_t, f_col, 0.0), axis=0, keepdims=True)
        m_prev = m_ref[d]
        d_intra = jnp.where(mask, b_col - b_row + i_row, -jnp.inf)
        d_inter = b_col + m_prev
        m_row = jnp.maximum(d_inter, jnp.max(d_intra, axis=1, keepdims=True))
        w_intra = jnp.exp(d_intra - m_row)
        w_inter = jnp.exp(d_inter - m_row)
        s = lax.dot_general(qb, kb, _NT, preferred_element_type=F32) * w_intra
        num = (jnp.dot(s.astype(BF16), vb, preferred_element_type=F32)
               + w_inter * jnp.dot(qb, c_ref[d].astype(BF16), preferred_element_type=F32))
        den = jnp.sum(s, axis=1, keepdims=True) + w_inter * jnp.sum(qf * n_ref[d], axis=1, keepdims=True)
        hh = num / jnp.maximum(jnp.abs(den), jnp.exp(-m_row))
        last = L - 1 if d == 0 else 0
        m_new = m_row[last:last + 1, :]
        b_last = b_col[last:last + 1, :]
        w_state = jnp.exp(b_last - b_col + i_col - m_new)
        decay = jnp.exp(b_last + m_prev - m_new)
        kw = w_state * kf
        c_ref[d] = decay * c_ref[d] + lax.dot_general(kw.astype(BF16), vb, (((0,), (0,)), ((), ())),
                                                      preferred_element_type=F32)
        n_ref[d] = decay * n_ref[d] + jnp.sum(kw, axis=0, keepdims=True)
        m_ref[d] = m_new
        if d == 0:
            hf_ref[rows, :] = hh
        else:
            hb_ref[rows, :] = hh

    def body(j, carry):
        chunk(0, j)
        chunk(1, nc - 1 - j)
        return carry

    lax.fori_loop(0, nc, body, 0)
    hs = hf_ref[...] + hb_ref[...]
    y = hs * lax.rsqrt(jnp.mean(hs * hs, axis=-1, keepdims=True) + NORM_EPS) * g_ref[...]
    y_ref[...] = (y * jax.nn.sigmoid(og_ref[...])).astype(y_ref.dtype)
    co_ref[...] = c_ref[...]
    no_ref[...] = n_ref[...]
    mo_ref[...] = m_ref[...]


def _mlstm(y_buf, proj, lay, gr, gc, bias, c0, n0, m0, gnorm, *, batch, seq, row0):
    nc = seq // MLSTM_CHUNK
    assert row0 % seq == 0
    rb0 = row0 // seq
    H = MLSTM_HEADS

    def pspec(name):
        cb = lay[name][0] // LANE
        return pl.BlockSpec((seq, LANE), lambda b, h: (rb0 + b, cb + h))

    st = lambda a, b_: pl.BlockSpec((None, 2, None, a, b_), lambda b, h: (b, 0, h, 0, 0))
    kern = functools.partial(_mlstm_kernel, nc=nc)

    def body(*refs):
        kern(*refs[:11], *refs[12:])

    return pl.pallas_call(
        body, grid=(batch, H),
        in_specs=[pl.BlockSpec(memory_space=pltpu.SMEM),
                  pspec("mq"), pspec("mk"), pspec("mv"), pspec("mo"),
                  pl.BlockSpec((None, nc, 4, MLSTM_CHUNK), lambda b, h: (h, rb0 + b, 0, 0)),
                  pl.BlockSpec((None, nc, MLSTM_CHUNK, 4), lambda b, h: (h, rb0 + b, 0, 0)),
                  st(MLSTM_DK, MLSTM_DV), st(1, MLSTM_DK), st(1, 1),
                  pl.BlockSpec((None, 1, MLSTM_DV), lambda b, h: (h, 0, 0)),
                  pl.BlockSpec(memory_space=pl.ANY)],
        out_specs=[pl.BlockSpec((seq, LANE), lambda b, h: (rb0 + b, h)),
                   st(MLSTM_DK, MLSTM_DV), st(1, MLSTM_DK), st(1, 1)],
        out_shape=[jax.ShapeDtypeStruct(y_buf.shape, y_buf.dtype),
                   jax.ShapeDtypeStruct((batch, 2, H, MLSTM_DK, MLSTM_DV), F32),
                   jax.ShapeDtypeStruct((batch, 2, H, 1, MLSTM_DK), F32),
                   jax.ShapeDtypeStruct((batch, 2, H, 1, 1), F32)],
        scratch_shapes=[pltpu.VMEM((seq, MLSTM_DV), F32), pltpu.VMEM((seq, MLSTM_DV), F32),
                        pltpu.VMEM((2, MLSTM_DK, MLSTM_DV), F32), pltpu.VMEM((2, 1, MLSTM_DK), F32),
                        pltpu.VMEM((2, 1, 1), F32)],
        input_output_aliases={11: 0},
        compiler_params=_params(("parallel", "arbitrary")))(
            bias, proj, proj, proj, proj, gr, gc, c0, n0, m0, gnorm, y_buf)


def _merge_kernel(yb_ref, ya_ref, yc_ref, m0_ref, m1_ref, m2_ref, wb_ref, wa_ref, wc_ref, o_ref):
    z = jax.nn.sigmoid(m0_ref[...]) * jnp.dot(ya_ref[...], wa_ref[...], preferred_element_type=F32)
    z = z + jax.nn.sigmoid(m1_ref[...]) * jnp.dot(yb_ref[...], wb_ref[...], preferred_element_type=F32)
    z = z + jax.nn.sigmoid(m2_ref[...]) * jnp.dot(yc_ref[...], wc_ref[...], preferred_element_type=F32)
    o_ref[...] = z.astype(o_ref.dtype)


def _merge(ya, yb, yc, proj, lay, wbr, d_model):
    n = ya.shape[0]
    tm, tn = _tile(n, 512), _tile(d_model, 512)
    moff = lay["merge"][0] // tn
    nj = d_model // tn
    gate = lambda br: pl.BlockSpec((tm, tn), lambda i, j: (i, moff + br * nj + j))
    return pl.pallas_call(
        _merge_kernel, grid=(n // tm, nj),
        in_specs=[pl.BlockSpec((tm, GQA_WIDTH), lambda i, j: (i, 0)), pl.BlockSpec((tm, MLA_WIDTH), lambda i, j: (i, 0)),
                  pl.BlockSpec((tm, MLSTM_WIDTH), lambda i, j: (i, 0)), gate(0), gate(1), gate(2),
                  pl.BlockSpec((GQA_WIDTH, tn), lambda i, j: (0, j)),
                  pl.BlockSpec((MLA_WIDTH, tn), lambda i, j: (GQA_WIDTH // MLA_WIDTH, j)),
                  pl.BlockSpec((MLSTM_WIDTH, tn), lambda i, j: ((GQA_WIDTH + MLA_WIDTH) // MLSTM_WIDTH, j))],
        out_specs=pl.BlockSpec((tm, tn), lambda i, j: (i, j)),
        out_shape=jax.ShapeDtypeStruct((n, d_model), BF16),
        compiler_params=_params(("parallel", "parallel")))(yb, ya, yc, proj, proj, proj, wbr, wbr, wbr)


def _mm_res_kernel(z_ref, w_ref, x_ref, g_ref, o_ref):
    o_ref[...] = x_ref[...] + g_ref[...] * jnp.dot(z_ref[...], w_ref[...], preferred_element_type=F32)


def _matmul_residual(z, w, x, mod3, g_blk, n_prompt, dec_seq):
    n, k = z.shape
    d = w.shape[1]
    tm = _tile(np.gcd(n_prompt, dec_seq), 1024)
    tn = _tile(d, 512)
    nj = d // tn
    mrow = functools.partial(_mod_row, tm=tm, n_prompt=n_prompt, dec_seq=dec_seq)
    return pl.pallas_call(
        _mm_res_kernel, grid=(n // tm, nj),
        in_specs=[pl.BlockSpec((tm, k), lambda i, j: (i, 0)), pl.BlockSpec((k, tn), lambda i, j: (0, j)),
                  pl.BlockSpec((tm, tn), lambda i, j: (i, j)),
                  pl.BlockSpec((None, 1, tn), lambda i, j: (mrow(i), 0, g_blk * nj + j))],
        out_specs=pl.BlockSpec((tm, tn), lambda i, j: (i, j)),
        out_shape=jax.ShapeDtypeStruct((n, d), F32),
        compiler_params=_params(("parallel", "parallel")))(z, w, x, mod3)


def _resid_kernel(x_ref, y_ref, g_ref, o_ref):
    o_ref[...] = x_ref[...] + g_ref[...] * y_ref[...]


def _residual(x, y, mod3, g_blk, n_prompt, dec_seq):
    n, d = x.shape
    tm = _tile(np.gcd(n_prompt, dec_seq), 256)
    mrow = functools.partial(_mod_row, tm=tm, n_prompt=n_prompt, dec_seq=dec_seq)
    return pl.pallas_call(
        _resid_kernel, grid=(n // tm,),
        in_specs=[pl.BlockSpec((tm, d), lambda i: (i, 0)), pl.BlockSpec((tm, d), lambda i: (i, 0)),
                  pl.BlockSpec((None, 1, d), lambda i: (mrow(i), 0, g_blk))],
        out_specs=pl.BlockSpec((tm, d), lambda i: (i, 0)),
        out_shape=jax.ShapeDtypeStruct((n, d), F32),
        compiler_params=_params(("parallel",)))(x, y, mod3)


def _router_kernel(x_ref, g_ref, sc_ref, sh_ref, rw_ref, rb_ref, h_ref, ti_ref, tg_ref, rk_ref, cnt_ref, carry_ref):
    @pl.when(pl.program_id(0) == 0)
    def _():
        carry_ref[...] = jnp.zeros_like(carry_ref)

    x = x_ref[...]
    r = lax.rsqrt(jnp.mean(x * x, axis=-1, keepdims=True) + NORM_EPS)
    h = (x * r * g_ref[...]) * (1.0 + sc_ref[...]) + sh_ref[...]
    hh = h.astype(BF16)
    h_ref[...] = hh
    hl = (h - hh.astype(F32)).astype(BF16)
    rw = rw_ref[...]
    rh = rw.astype(BF16)
    rl = (rw - rh.astype(F32)).astype(BF16)
    lg = (lax.dot_general(rh, hh, _NT, preferred_element_type=F32)
          + lax.dot_general(rh, hl, _NT, preferred_element_type=F32)
          + lax.dot_general(rl, hh, _NT, preferred_element_type=F32)) + rb_ref[...]
    n_exp, tm = lg.shape
    eidx = lax.broadcasted_iota(I32, lg.shape, 0).astype(F32)
    vals, idxs = [], []
    cur = lg
    for _k in range(TOP_K):
        m = jnp.max(cur, axis=0, keepdims=True)
        ix = jnp.min(jnp.where(cur == m, eidx, float(n_exp)), axis=0, keepdims=True)
        vals.append(m)
        idxs.append(ix)
        cur = jnp.where(eidx == ix, -jnp.inf, cur)
    ex = [jnp.exp(v - vals[0]) for v in vals]
    den = ex[0] + ex[1] + ex[2] + ex[3]
    onehot = jnp.zeros(lg.shape, F32)
    for ix in idxs:
        onehot = onehot + (eidx == ix).astype(F32)
    ti = lax.broadcasted_iota(I32, (tm, tm), 0)
    tj = lax.broadcasted_iota(I32, (tm, tm), 1)
    tri = (ti <= tj).astype(BF16)
    cum = jnp.dot(onehot.astype(BF16), tri, preferred_element_type=F32)
    excl = cum - onehot + carry_ref[...]
    for k in range(TOP_K):
        ti_ref[k:k + 1, :] = idxs[k].astype(I32)
        tg_ref[k:k + 1, :] = ex[k] / den
        rk_ref[k:k + 1, :] = jnp.sum(jnp.where(eidx == idxs[k], excl, 0.0), axis=0, keepdims=True).astype(I32)
    carry_ref[...] = carry_ref[...] + jnp.sum(onehot, axis=1, keepdims=True)
    cnt_ref[...] = carry_ref[...]


def _router(x, g, mod3, sc_blk, sh_blk, rw_t, rb, n_prompt, dec_seq):
    n, d = x.shape
    n_exp = rw_t.shape[0]
    tm = _tile(np.gcd(n_prompt, dec_seq), 256)
    mrow = functools.partial(_mod_row, tm=tm, n_prompt=n_prompt, dec_seq=dec_seq)
    kspec = pl.BlockSpec((TOP_K, tm), lambda i: (0, i))
    return pl.pallas_call(
        _router_kernel, grid=(n // tm,),
        in_specs=[pl.BlockSpec((tm, d), lambda i: (i, 0)), pl.BlockSpec((1, d), lambda i: (0, 0)),
                  pl.BlockSpec((None, 1, d), lambda i: (mrow(i), 0, sc_blk)),
                  pl.BlockSpec((None, 1, d), lambda i: (mrow(i), 0, sh_blk)),
                  pl.BlockSpec((n_exp, d), lambda i: (0, 0)), pl.BlockSpec((n_exp, 1), lambda i: (0, 0))],
        out_specs=[pl.BlockSpec((tm, d), lambda i: (i, 0)), kspec, kspec, kspec,
                   pl.BlockSpec((n_exp, 1), lambda i: (0, 0))],
        out_shape=[jax.ShapeDtypeStruct((n, d), BF16), jax.ShapeDtypeStruct((TOP_K, n), I32),
                   jax.ShapeDtypeStruct((TOP_K, n), F32), jax.ShapeDtypeStruct((TOP_K, n), I32),
                   jax.ShapeDtypeStruct((n_exp, 1), F32)],
        scratch_shapes=[pltpu.VMEM((n_exp, 1), F32)],
        compiler_params=_params(("arbitrary",)))(x, g.reshape(1, d), mod3, mod3, rw_t, rb.reshape(n_exp, 1))


def _gather_kernel(idx_ref, src_ref, out_ref, sem, *, rows):
    base = pl.program_id(0) * rows

    def issue(r, carry):
        pltpu.make_async_copy(src_ref.at[idx_ref[0, r]], out_ref.at[base + r], sem).start()
        return carry

    lax.fori_loop(0, rows, issue, 0)

    def drain(r, carry):
        pltpu.make_async_copy(src_ref.at[0], out_ref.at[base + r], sem).wait()
        return carry

    lax.fori_loop(0, rows, drain, 0)


def _gather_rows(src3, idx):
    n_out = idx.shape[0]
    rows = _tile(n_out, GATHER_ROWS)
    return pl.pallas_call(
        functools.partial(_gather_kernel, rows=rows), grid=(n_out // rows,),
        in_specs=[pl.BlockSpec((None, 1, rows), lambda i: (i, 0, 0), memory_space=pltpu.SMEM),
                  pl.BlockSpec(memory_space=pl.ANY)],
        out_specs=pl.BlockSpec(memory_space=pl.ANY),
        out_shape=jax.ShapeDtypeStruct((n_out,) + src3.shape[1:], src3.dtype),
        scratch_shapes=[pltpu.SemaphoreType.DMA(())],
        compiler_params=_params(("arbitrary",)))(idx.reshape(n_out // rows, 1, rows), src3)


def _ffn_kernel(be_ref, na_ref, x_ref, wg_ref, wu_ref, wd_ref, bg_ref, bu_ref, bd_ref, o_ref):
    j, f = pl.program_id(0), pl.program_id(1)

    @pl.when(j < na_ref[0])
    def _():
        x = x_ref[...]
        g = jnp.dot(x, wg_ref[...], preferred_element_type=F32) + bg_ref[...]
        u = jnp.dot(x, wu_ref[...], preferred_element_type=F32) + bu_ref[...]
        g = jnp.minimum(g, SWIGLU_LIMIT)
        u = jnp.clip(u, -SWIGLU_LIMIT, SWIGLU_LIMIT)
        a = ((u + 1.0) * g * jax.nn.sigmoid(SWIGLU_ALPHA * g)).astype(BF16)
        y = jnp.dot(a, wd_ref[...], preferred_element_type=F32)

        @pl.when(f == 0)
        def _():
            o_ref[...] = y + bd_ref[...]

        @pl.when(f > 0)
        def _():
            o_ref[...] += y


def _expert_ffn(xs, blk_expert, n_active, wg, wu, wd, bg, bu, bd, layer):
    n_slots, d = xs.shape
    n_exp, _, d_ff = wg.shape[1:]
    bm = MOE_ROWS
    tf = _tile(d_ff, 256)
    nf = d_ff // tf
    nb = n_slots // bm

    def jj(j, na):
        return jnp.minimum(j, na[0] - 1)

    def ff(j, f, na):
        return jnp.where(j < na[0], f, nf - 1)

    grid_spec = pltpu.PrefetchScalarGridSpec(
        num_scalar_prefetch=2, grid=(nb, nf),
        in_specs=[pl.BlockSpec((bm, d), lambda j, f, be, na: (jj(j, na), 0)),
                  pl.BlockSpec((None, None, d, tf), lambda j, f, be, na: (layer, be[jj(j, na)], 0, ff(j, f, na))),
                  pl.BlockSpec((None, None, d, tf), lambda j, f, be, na: (layer, be[jj(j, na)], 0, ff(j, f, na))),
                  pl.BlockSpec((None, None, tf, d), lambda j, f, be, na: (layer, be[jj(j, na)], ff(j, f, na), 0)),
                  pl.BlockSpec((None, None, 1, tf), lambda j, f, be, na: (layer, be[jj(j, na)], 0, ff(j, f, na))),
                  pl.BlockSpec((None, None, 1, tf), lambda j, f, be, na: (layer, be[jj(j, na)], 0, ff(j, f, na))),
                  pl.BlockSpec((None, None, 1, d), lambda j, f, be, na: (layer, be[jj(j, na)], 0, 0))],
        out_specs=pl.BlockSpec((bm, d), lambda j, f, be, na: (jj(j, na), 0)))
    return pl.pallas_call(
        _ffn_kernel, grid_spec=grid_spec,
        out_shape=jax.ShapeDtypeStruct((n_slots, d), F32),
        compiler_params=_params(("arbitrary", "arbitrary")))(blk_expert, n_active, xs, wg, wu, wd, bg, bu, bd)


def _combine_kernel(g_ref, w_ref, o_ref):
    acc = g_ref[0] * w_ref[0]
    for k in range(1, TOP_K):
        acc = acc + g_ref[k] * w_ref[k]
    o_ref[...] = acc


def _combine(gathered, gates):
    _, n, s, _ = gathered.shape
    tm = _tile(n, 128)
    return pl.pallas_call(
        _combine_kernel, grid=(n // tm,),
        in_specs=[pl.BlockSpec((TOP_K, tm, s, LANE), lambda i: (0, i, 0, 0)),
                  pl.BlockSpec((TOP_K, tm, 1, 1), lambda i: (0, i, 0, 0))],
        out_specs=pl.BlockSpec((tm, s, LANE), lambda i: (i, 0, 0)),
        out_shape=jax.ShapeDtypeStruct((n, s, LANE), F32),
        compiler_params=_params(("parallel",)))(gathered, gates)


def _moe(x, p, layer, mod3, n_prompt, dec_seq):
    n, d = x.shape
    n_exp = p["router_w"].shape[-1]
    bm = MOE_ROWS
    h, topi, gate, rank, cnt = _router(x, p["norm_ffn"][layer], mod3, 4, 3, p["router_w"][layer].T,
                                       p["router_b"][layer], n_prompt, dec_seq)
    counts = cnt[:, 0].astype(I32)
    padded = (counts + bm - 1) // bm * bm
    pad_end = jnp.cumsum(padded)
    pad_start = pad_end - padded
    dest = pad_start[topi] + rank
    n_slots = n * TOP_K + n_exp * bm
    nb = n_slots // bm
    slot_tok = jnp.zeros((n_slots,), I32).at[dest.reshape(-1)].set(jnp.tile(jnp.arange(n, dtype=I32), TOP_K))
    blk_expert = jnp.minimum(jnp.searchsorted(pad_end, jnp.arange(nb, dtype=I32) * bm, side="right"),
                             n_exp - 1).astype(I32)
    n_active = (pad_end[-1:] // bm).astype(I32)
    s = d // LANE
    xs = _gather_rows(h.reshape(n, s, LANE), slot_tok).reshape(n_slots, d)
    yb = _expert_ffn(xs, blk_expert, n_active, p["moe_w_gate"], p["moe_w_up"], p["moe_w_down"],
                     p["moe_b_gate"], p["moe_b_up"], p["moe_b_down"], layer)
    got = _gather_rows(yb.reshape(n_slots, s, LANE), dest.reshape(-1)).reshape(TOP_K, n, s, LANE)
    y = _combine(got, gate.reshape(TOP_K, n, 1, 1)).reshape(n, d)
    return _residual(x, y, mod3, 5, n_prompt, dec_seq)


def kernel(x_prompt, x_sample, cache_mla_ckv, cache_mla_krope, cache_gqa_k, cache_gqa_v, state_mlstm_C, state_mlstm_n, state_mlstm_m, c, c_ctx, w_mod, b_mod, norm_attn, norm_ffn, w_in, mla_q_norm, mla_w_qb, mla_kv_norm, mla_w_kvb, gqa_q_norm, gqa_k_norm, mlstm_gate_bias, mlstm_out_norm, w_branch, w_out, router_w, router_b, moe_w_gate, moe_b_gate, moe_w_up, moe_b_up, moe_w_down, moe_b_down, final_norm):
    bp, sp, d = x_prompt.shape
    bs, ss, _ = x_sample.shape
    depth = w_mod.shape[0]
    past = cache_mla_ckv.shape[2]
    n_p, n_s = bp * sp, bs * ss
    n = n_p + n_s
    tk_s = ss + past
    H = MLSTM_HEADS
    lay, _ = _proj_layout(d)
    n_exp = router_w.shape[-1]

    x = jnp.concatenate([x_prompt.reshape(n_p, d), x_sample.reshape(n_s, d)], axis=0)
    cond = jnp.zeros((16, d), F32).at[0].set(c_ctx).at[1:1 + bs].set(c)
    tabs = _rope_tables(n_p, bs, ss, MLA_ROPE) + _rope_tables(n_p, bs, ss, GQA_HEAD_DIM)
    moe = dict(router_w=router_w, router_b=router_b, norm_ffn=norm_ffn,
               moe_w_gate=moe_w_gate.astype(BF16), moe_w_up=moe_w_up.astype(BF16), moe_w_down=moe_w_down.astype(BF16),
               moe_b_gate=moe_b_gate.reshape(depth, n_exp, 1, -1), moe_b_up=moe_b_up.reshape(depth, n_exp, 1, -1),
               moe_b_down=moe_b_down.reshape(depth, n_exp, 1, d))
    zc = jnp.zeros((bp, 2, H, MLSTM_DK, MLSTM_DV), F32)
    zn = jnp.zeros((bp, 2, H, 1, MLSTM_DK), F32)
    zm = jnp.zeros((bp, 2, H, 1, 1), F32)
    ctx = []
    for l in range(depth):
        mod3 = _modulation(cond, w_mod[l], b_mod[l]).reshape(16, 1, 6 * d)
        h = _norm_mod(x, norm_attn[l], mod3, 1, 0, n_p, ss)
        proj = _matmul(h, _build_w_in(w_in[l], d), F32, 1024, 512)
        qn = jnp.pad(mla_q_norm[l], (0, Q_LORA_PAD - MLA_Q_LORA)).reshape(1, Q_LORA_PAD)
        qm, ckv32, ckv16, kr16, qg, kg32, kg16, v16 = _prep(
            proj, lay, tabs, qn, mla_kv_norm[l].reshape(1, -1), gqa_q_norm[l].reshape(1, -1),
            gqa_k_norm[l].reshape(1, -1), _build_w_qb(mla_w_qb[l]))
        ckv_all = jnp.concatenate([
            jnp.concatenate([ckv16[n_p:].reshape(bs, ss, -1), cache_mla_ckv[:, l].astype(BF16)], axis=1).reshape(bs * tk_s, -1),
            ckv16[:n_p]], axis=0)
        kv = _matmul(ckv_all, _build_w_kvb(mla_w_kvb[l]), BF16, 1024, 512)
        kr_s = jnp.concatenate([kr16[n_p:].reshape(bs, ss, LANE),
                                jnp.pad(cache_mla_krope[:, l], ((0, 0), (0, 0), (0, LANE - MLA_ROPE))).astype(BF16)],
                               axis=1).reshape(bs * tk_s, LANE)
        kg_s = jnp.concatenate([kg16[n_p:].reshape(bs, ss, -1), cache_gqa_k[:, l].reshape(bs, past, -1).astype(BF16)],
                               axis=1).reshape(bs * tk_s, -1)
        vg_s = jnp.concatenate([v16[n_p:].reshape(bs, ss, -1), cache_gqa_v[:, l].reshape(bs, past, -1).astype(BF16)],
                               axis=1).reshape(bs * tk_s, -1)
        ya = jnp.zeros((n, MLA_WIDTH), BF16)
        ya = _attention_call(ya, qm, kv, kv, batch=bp, tq_len=sp, tk_len=sp, heads=MLA_HEADS, group=1, q_row0=0,
                             k_row0=bs * tk_s, q_col0=0, k_col0=0, v_col0=MLA_HEADS, q2_col0=MLA_HEADS, k2=kr16,
                             k2_row0=0)
        ya = _attention_call(ya, qm, kv, kv, batch=bs, tq_len=ss, tk_len=tk_s, heads=MLA_HEADS, group=1, q_row0=n_p,
                             k_row0=0, q_col0=0, k_col0=0, v_col0=MLA_HEADS, q2_col0=MLA_HEADS, k2=kr_s, k2_row0=0)
        grp = GQA_HEADS // GQA_KV_HEADS
        yb = jnp.zeros((n, GQA_WIDTH), BF16)
        yb = _attention_call(yb, qg, kg16, v16, batch=bp, tq_len=sp, tk_len=sp, heads=GQA_HEADS,
                             group=grp, q_row0=0, k_row0=0, q_col0=0, k_col0=0, v_col0=0)
        yb = _attention_call(yb, qg, kg_s, vg_s, batch=bs, tq_len=ss, tk_len=tk_s, heads=GQA_HEADS, group=grp,
                             q_row0=n_p, k_row0=0, q_col0=0, k_col0=0, v_col0=0)
        mg0 = lay["mg"][0]
        mg = proj[:, mg0:mg0 + 4 * H].reshape(n // MLSTM_CHUNK, MLSTM_CHUNK, 4, H)
        gr = mg.transpose(3, 0, 2, 1)
        gc = mg.transpose(3, 0, 1, 2)
        bias = mlstm_gate_bias[l].reshape(-1)
        gnorm = mlstm_out_norm[l].reshape(H, 1, MLSTM_DV)
        yc = jnp.zeros((n, MLSTM_WIDTH), BF16)
        yc, c_new, n_new, m_new = _mlstm(yc, proj, lay, gr, gc, bias, zc, zn, zm, gnorm, batch=bp, seq=sp, row0=0)
        yc, _, _, _ = _mlstm(yc, proj, lay, gr, gc, bias, state_mlstm_C[:, l],
                             state_mlstm_n[:, l].reshape(bs, 2, H, 1, MLSTM_DK),
                             state_mlstm_m[:, l].reshape(bs, 2, H, 1, 1), gnorm, batch=bs, seq=ss, row0=n_p)
        wb = w_branch[l].astype(BF16)
        wbr = jnp.concatenate([wb[MLA_WIDTH:MLA_WIDTH + GQA_WIDTH], wb[:MLA_WIDTH], wb[MLA_WIDTH + GQA_WIDTH:]], axis=0)
        z = _merge(ya, yb, yc, proj, lay, wbr, d)
        x = _matmul_residual(z, w_out[l].astype(BF16), x, mod3, 2, n_p, ss)
        x = _moe(x, moe, l, mod3, n_p, ss)
        kr0, gv0 = lay["kr"][0], lay["gv"][0]
        ctx.append((ckv32[:n_p].reshape(bp, sp, -1), proj[:n_p, kr0:kr0 + MLA_ROPE].reshape(bp, sp, MLA_ROPE),
                    kg32[:n_p].reshape(bp, sp, GQA_KV_HEADS, GQA_HEAD_DIM),
                    proj[:n_p, gv0:gv0 + GQA_KV_WIDTH].reshape(bp, sp, GQA_KV_HEADS, GQA_HEAD_DIM),
                    c_new, n_new.reshape(bp, 2, H, MLSTM_DK), m_new.reshape(bp, 2, H)))
    y = _rmsnorm(x, final_norm)
    outs = [jnp.stack([t[i] for t in ctx], axis=1) for i in range(7)]
    return (y[:n_p].reshape(bp, sp, d), y[n_p:].reshape(bs, ss, d), *outs)
```

```python
import functools

import jax
import jax.numpy as jnp
import numpy as np
from jax import lax
from jax.experimental import pallas as pl
from jax.experimental.pallas import tpu as pltpu

F32 = jnp.float32
BF16 = jnp.bfloat16
I32 = jnp.int32

GRID_W = 64
ROPE_BASE = 10000.0
NORM_EPS = 1e-6

MLA_HEADS = 8
MLA_Q_LORA = 896
MLA_KV_LORA = 512
MLA_NOPE = 128
MLA_ROPE = 64
MLA_V = 128
GQA_HEADS = 16
GQA_KV_HEADS = 4
GQA_HEAD_DIM = 128
MLSTM_HEADS = 8
MLSTM_DK = 128
MLSTM_DV = 128
MLSTM_CHUNK = 64
N_BRANCHES = 3
TOP_K = 4
SWIGLU_LIMIT = 7.0
SWIGLU_ALPHA = 1.702

LANE = 128
VMEM_LIMIT = 56 * 1024 * 1024
MOE_ROWS = 512
DMA_UNROLL = 8
MLSTM_VMEM_BUDGET = 32 * 1024 * 1024

MLA_WIDTH = MLA_HEADS * MLA_V
GQA_WIDTH = GQA_HEADS * GQA_HEAD_DIM
GQA_KV_WIDTH = GQA_KV_HEADS * GQA_HEAD_DIM
MLSTM_WIDTH = MLSTM_HEADS * MLSTM_DV
Q_LORA_PAD = 1024


def _tile(n, cap):
    for t in (2048, 1024, 512, 256, 128, 64, 32, 16, 8):
        if t <= cap and n % t == 0:
            return t
    raise ValueError(f"no tile for {n}")


def _params(sem, vmem=VMEM_LIMIT):
    return pltpu.CompilerParams(dimension_semantics=sem, vmem_limit_bytes=vmem)


def _proj_layout(d_model):
    segs = [("merge", N_BRANCHES * d_model, N_BRANCHES * d_model), ("gq", GQA_WIDTH, GQA_WIDTH),
            ("mq", MLSTM_WIDTH, MLSTM_WIDTH), ("mk", MLSTM_WIDTH, MLSTM_WIDTH), ("mv", MLSTM_WIDTH, MLSTM_WIDTH),
            ("mo", MLSTM_WIDTH, MLSTM_WIDTH), ("qa", MLA_Q_LORA, Q_LORA_PAD), ("kva", MLA_KV_LORA, MLA_KV_LORA),
            ("gk", GQA_KV_WIDTH, GQA_KV_WIDTH), ("gv", GQA_KV_WIDTH, GQA_KV_WIDTH),
            ("kr", MLA_ROPE, LANE), ("mg", 4 * MLSTM_HEADS, LANE)]
    off, out = 0, {}
    for name, w, wp in segs:
        assert off % wp == 0, (name, off, wp)
        out[name] = (off, w, wp)
        off += wp
    total = -(-off // 512) * 512
    return out, total


def _build_w_in(w_in_l, d_model):
    lay, total = _proj_layout(d_model)
    ref_sizes = dict(qa=MLA_Q_LORA, kva=MLA_KV_LORA, kr=MLA_ROPE, gq=GQA_WIDTH, gk=GQA_KV_WIDTH, gv=GQA_KV_WIDTH,
                     mq=MLSTM_WIDTH, mk=MLSTM_WIDTH, mv=MLSTM_WIDTH, mo=MLSTM_WIDTH, mg=4 * MLSTM_HEADS,
                     merge=N_BRANCHES * d_model)
    ref_order = ["qa", "kva", "kr", "gq", "gk", "gv", "mq", "mk", "mv", "mo", "mg", "merge"]
    src, o = {}, 0
    for name in ref_order:
        src[name] = (o, ref_sizes[name])
        o += ref_sizes[name]
    wb = w_in_l.astype(BF16)
    pieces, used = [], 0
    for name, (off, w, wp) in lay.items():
        s0, sw = src[name]
        pieces.append(wb[:, s0:s0 + sw])
        if wp > w:
            pieces.append(jnp.zeros((wb.shape[0], wp - w), BF16))
        used = off + wp
    if total > used:
        pieces.append(jnp.zeros((wb.shape[0], total - used), BF16))
    return jnp.concatenate(pieces, axis=1)


def _build_w_qb(w_qb_l):
    w = w_qb_l.astype(BF16).reshape(MLA_Q_LORA, MLA_HEADS, MLA_NOPE + MLA_ROPE)
    nope = w[:, :, :MLA_NOPE].reshape(MLA_Q_LORA, MLA_HEADS * MLA_NOPE)
    rope = jnp.pad(w[:, :, MLA_NOPE:], ((0, 0), (0, 0), (0, LANE - MLA_ROPE))).reshape(MLA_Q_LORA, MLA_HEADS * LANE)
    w2 = jnp.concatenate([nope, rope], axis=1)
    return jnp.pad(w2, ((0, Q_LORA_PAD - MLA_Q_LORA), (0, 0)))


def _build_w_kvb(w_kvb_l):
    w = w_kvb_l.astype(BF16).reshape(MLA_KV_LORA, MLA_HEADS, MLA_NOPE + MLA_V)
    return jnp.concatenate([w[:, :, :MLA_NOPE].reshape(MLA_KV_LORA, -1), w[:, :, MLA_NOPE:].reshape(MLA_KV_LORA, -1)],
                           axis=1)


def _rope_tables(n_prompt, dec_batch, dec_seq, rot_dim):
    quarter = rot_dim // 4
    grid_rows = dec_seq // GRID_W
    row = jnp.repeat(jnp.arange(grid_rows, dtype=F32), GRID_W)
    col = jnp.tile(jnp.arange(GRID_W, dtype=F32), grid_rows)
    inv_freq = ROPE_BASE ** (-jnp.arange(quarter, dtype=F32) / quarter)
    ang_r = row[:, None] * inv_freq
    ang_c = col[:, None] * inv_freq
    ang = jnp.concatenate([ang_r, ang_r, ang_c, ang_c], axis=-1)
    cos = jnp.pad(jnp.cos(ang), ((0, 0), (0, LANE - rot_dim)), constant_values=1.0)
    sin = jnp.pad(jnp.sin(ang), ((0, 0), (0, LANE - rot_dim)))
    cos = jnp.concatenate([jnp.ones((n_prompt, LANE), F32), jnp.tile(cos, (dec_batch, 1))], axis=0)
    sin = jnp.concatenate([jnp.zeros((n_prompt, LANE), F32), jnp.tile(sin, (dec_batch, 1))], axis=0)
    return cos, sin


def _mod_kernel(c_ref, w_ref, b_ref, o_ref):
    c = c_ref[...]
    a = (c * jax.nn.sigmoid(c)).astype(BF16)
    o_ref[...] = jnp.dot(a, w_ref[...].astype(BF16), preferred_element_type=F32) + b_ref[...]


def _modulation(cond, w_mod_l, b_mod_l):
    rows, d = cond.shape
    n = w_mod_l.shape[1]
    tn = _tile(n, 512)
    return pl.pallas_call(
        _mod_kernel, grid=(n // tn,),
        in_specs=[pl.BlockSpec((rows, d), lambda j: (0, 0)), pl.BlockSpec((d, tn), lambda j: (0, j)),
                  pl.BlockSpec((1, tn), lambda j: (0, j))],
        out_specs=pl.BlockSpec((rows, tn), lambda j: (0, j)),
        out_shape=jax.ShapeDtypeStruct((rows, n), F32),
        compiler_params=_params(("parallel",)))(cond, w_mod_l, b_mod_l.reshape(1, n))


def _mm_kernel(x_ref, w_ref, o_ref):
    o_ref[...] = jnp.dot(x_ref[...], w_ref[...], preferred_element_type=F32).astype(o_ref.dtype)


def _matmul(x, w, out_dtype, tm_cap, tn_cap):
    m, k = x.shape
    n = w.shape[1]
    tm, tn = _tile(m, tm_cap), _tile(n, tn_cap)
    return pl.pallas_call(
        _mm_kernel, grid=(m // tm, n // tn),
        in_specs=[pl.BlockSpec((tm, k), lambda i, j: (i, 0)), pl.BlockSpec((k, tn), lambda i, j: (0, j))],
        out_specs=pl.BlockSpec((tm, tn), lambda i, j: (i, j)),
        out_shape=jax.ShapeDtypeStruct((m, n), out_dtype),
        compiler_params=_params(("parallel", "parallel")))(x, w)


def _mod_row(i, tm, n_prompt, dec_seq):
    r0 = i * tm
    return jnp.where(r0 < n_prompt, 0, 1 + (r0 - n_prompt) // dec_seq)


def _norm_mod_kernel(x_ref, g_ref, sc_ref, sh_ref, o_ref):
    x = x_ref[...]
    r = lax.rsqrt(jnp.mean(x * x, axis=-1, keepdims=True) + NORM_EPS)
    y = x * r * g_ref[...]
    o_ref[...] = (y * (1.0 + sc_ref[...]) + sh_ref[...]).astype(o_ref.dtype)


def _norm_mod(x, g, mod3, sc_blk, sh_blk, n_prompt, dec_seq):
    n, d = x.shape
    tm = _tile(np.gcd(n_prompt, dec_seq), 256)
    mrow = functools.partial(_mod_row, tm=tm, n_prompt=n_prompt, dec_seq=dec_seq)
    return pl.pallas_call(
        _norm_mod_kernel, grid=(n // tm,),
        in_specs=[pl.BlockSpec((tm, d), lambda i: (i, 0)), pl.BlockSpec((1, d), lambda i: (0, 0)),
                  pl.BlockSpec((None, 1, d), lambda i: (mrow(i), 0, sc_blk)),
                  pl.BlockSpec((None, 1, d), lambda i: (mrow(i), 0, sh_blk))],
        out_specs=pl.BlockSpec((tm, d), lambda i: (i, 0)),
        out_shape=jax.ShapeDtypeStruct((n, d), BF16),
        compiler_params=_params(("parallel",)))(x, g.reshape(1, d), mod3, mod3)


def _rms_kernel(x_ref, g_ref, o_ref):
    x = x_ref[...]
    r = lax.rsqrt(jnp.mean(x * x, axis=-1, keepdims=True) + NORM_EPS)
    o_ref[...] = x * r * g_ref[...]


def _rmsnorm(x, g):
    n, d = x.shape
    tm = _tile(n, 256)
    return pl.pallas_call(
        _rms_kernel, grid=(n // tm,),
        in_specs=[pl.BlockSpec((tm, d), lambda i: (i, 0)), pl.BlockSpec((1, d), lambda i: (0, 0))],
        out_specs=pl.BlockSpec((tm, d), lambda i: (i, 0)),
        out_shape=jax.ShapeDtypeStruct((n, d), F32),
        compiler_params=_params(("parallel",)))(x, g.reshape(1, d))


def _rot_half(x, half):
    lane = lax.broadcasted_iota(I32, x.shape, 1)
    first = (lane % (2 * half)) < half
    return jnp.where(first, -pltpu.roll(x, LANE - half, 1), pltpu.roll(x, half, 1))


def _prep_kernel(qa_ref, kva_ref, gq_ref, gk_ref, gv_ref, kr_ref, ca_ref, sa_ref, cb_ref, sb_ref,
                 qn_ref, kvn_ref, gqn_ref, gkn_ref, wqb_ref,
                 qm_ref, ckv32_ref, ckv16_ref, kr16_ref, qg_ref, kg32_ref, kg16_ref, v16_ref):
    ca, sa, cb, sb = ca_ref[...], sa_ref[...], cb_ref[...], sb_ref[...]
    qa = qa_ref[...]
    r = lax.rsqrt(jnp.sum(qa * qa, axis=-1, keepdims=True) * (1.0 / MLA_Q_LORA) + NORM_EPS)
    qn = (qa * r * qn_ref[...]).astype(BF16)
    q = jnp.dot(qn, wqb_ref[...], preferred_element_type=F32)
    scale_a = (MLA_NOPE + MLA_ROPE) ** -0.5
    nw = MLA_HEADS * MLA_NOPE
    qm_ref[:, :nw] = (q[:, :nw] * scale_a).astype(BF16)
    for h in range(MLA_HEADS):
        x = q[:, nw + h * LANE: nw + (h + 1) * LANE]
        qm_ref[:, nw + h * LANE: nw + (h + 1) * LANE] = (
            (x * ca + _rot_half(x, MLA_ROPE // 4) * sa) * scale_a).astype(BF16)
    kva = kva_ref[...]
    r = lax.rsqrt(jnp.mean(kva * kva, axis=-1, keepdims=True) + NORM_EPS)
    ckv = kva * r * kvn_ref[...]
    ckv32_ref[...] = ckv
    ckv16_ref[...] = ckv.astype(BF16)
    kr = kr_ref[...]
    kr16_ref[...] = (kr * ca + _rot_half(kr, MLA_ROPE // 4) * sa).astype(BF16)
    scale_b = GQA_HEAD_DIM ** -0.5
    gqn, gkn = gqn_ref[...], gkn_ref[...]
    for h in range(GQA_HEADS):
        sl = slice(h * LANE, (h + 1) * LANE)
        x = gq_ref[:, sl]
        x = x * lax.rsqrt(jnp.mean(x * x, axis=-1, keepdims=True) + NORM_EPS) * gqn
        qg_ref[:, sl] = ((x * cb + _rot_half(x, GQA_HEAD_DIM // 4) * sb) * scale_b).astype(BF16)
    for h in range(GQA_KV_HEADS):
        sl = slice(h * LANE, (h + 1) * LANE)
        x = gk_ref[:, sl]
        x = x * lax.rsqrt(jnp.mean(x * x, axis=-1, keepdims=True) + NORM_EPS) * gkn
        kg32_ref[:, sl] = x
        kg16_ref[:, sl] = (x * cb + _rot_half(x, GQA_HEAD_DIM // 4) * sb).astype(BF16)
    v16_ref[...] = gv_ref[...].astype(BF16)


def _prep(proj, lay, tabs, qn, kvn, gqn, gkn, wqb):
    n = proj.shape[0]
    tm = _tile(n, 256)
    ca, sa, cb, sb = tabs

    def pspec(name):
        off, _, wp = lay[name]
        return pl.BlockSpec((tm, wp), lambda i: (i, off // wp))

    def row(w):
        return pl.BlockSpec((tm, w), lambda i: (i, 0))

    def full(a):
        return pl.BlockSpec(a.shape, lambda i: (0,) * a.ndim)

    outs = [(2 * MLA_HEADS * LANE, BF16), (MLA_KV_LORA, F32), (MLA_KV_LORA, BF16), (LANE, BF16),
            (GQA_WIDTH, BF16), (GQA_KV_WIDTH, F32), (GQA_KV_WIDTH, BF16), (GQA_KV_WIDTH, BF16)]
    return pl.pallas_call(
        _prep_kernel, grid=(n // tm,),
        in_specs=[pspec("qa"), pspec("kva"), pspec("gq"), pspec("gk"), pspec("gv"), pspec("kr"),
                  row(LANE), row(LANE), row(LANE), row(LANE), full(qn), full(kvn), full(gqn), full(gkn), full(wqb)],
        out_specs=[row(w) for w, _ in outs],
        out_shape=[jax.ShapeDtypeStruct((n, w), dt) for w, dt in outs],
        compiler_params=_params(("parallel",)))(proj, proj, proj, proj, proj, proj, ca, sa, cb, sb,
                                                 qn, kvn, gqn, gkn, wqb)


_NT = (((1,), (1,)), ((), ()))


def _attn_kernel(*refs, hp, shared_kv, two_part):
    if two_part:
        q1_ref, q2_ref, k1_ref, k2_ref, v_ref, _, o_ref = refs
    else:
        q1_ref, k1_ref, v_ref, _, o_ref = refs
    for a in range(hp):
        sl = slice(a * LANE, (a + 1) * LANE)
        ksl = slice(0, LANE) if shared_kv else sl
        s = lax.dot_general(q1_ref[:, sl], k1_ref[:, ksl], _NT, preferred_element_type=F32)
        if two_part:
            s = s + lax.dot_general(q2_ref[:, sl], k2_ref[...], _NT, preferred_element_type=F32)
        m = jnp.max(s, axis=-1, keepdims=True)
        p = jnp.exp(s - m)
        l = jnp.sum(p, axis=-1, keepdims=True)
        o = jnp.dot(p.astype(BF16), v_ref[:, ksl], preferred_element_type=F32)
        o_ref[:, sl] = (o / l).astype(o_ref.dtype)


def _attention_call(out_buf, q, k, v, *, batch, tq_len, tk_len, heads, hp, tq_cap, shared_kv, q_row0, k_row0,
                    q_col0, k_col0, v_col0, q2_col0=None, k2=None, k2_row0=0):
    tq = _tile(tq_len, tq_cap)
    nq = tq_len // tq
    assert q_row0 % tq == 0 and k_row0 % tk_len == 0 and k2_row0 % tk_len == 0 and heads % hp == 0
    assert q_col0 % hp == 0 and (shared_kv or (k_col0 % hp == 0 and v_col0 % hp == 0))
    qb0, kb0, k2b0 = q_row0 // tq, k_row0 // tk_len, k2_row0 // tk_len
    two = k2 is not None
    qspec = lambda c0: pl.BlockSpec((tq, hp * LANE), lambda b, g, i: (qb0 + b * nq + i, c0 // hp + g))
    if shared_kv:
        kspec = lambda c0: pl.BlockSpec((tk_len, LANE), lambda b, g, i: (kb0 + b, c0 + g))
    else:
        kspec = lambda c0: pl.BlockSpec((tk_len, hp * LANE), lambda b, g, i: (kb0 + b, c0 // hp + g))
    in_specs, args = [qspec(q_col0)], [q]
    if two:
        in_specs.append(qspec(q2_col0))
        args.append(q)
    in_specs.append(kspec(k_col0))
    args.append(k)
    if two:
        in_specs.append(pl.BlockSpec((tk_len, LANE), lambda b, g, i: (k2b0 + b, 0)))
        args.append(k2)
    in_specs += [kspec(v_col0), pl.BlockSpec(memory_space=pl.ANY)]
    args += [v, out_buf]
    return pl.pallas_call(
        functools.partial(_attn_kernel, hp=hp, shared_kv=shared_kv, two_part=two),
        grid=(batch, heads // hp, nq), in_specs=in_specs,
        out_specs=pl.BlockSpec((tq, hp * LANE), lambda b, g, i: (qb0 + b * nq + i, g)),
        out_shape=jax.ShapeDtypeStruct(out_buf.shape, out_buf.dtype),
        input_output_aliases={len(args) - 1: 0},
        compiler_params=_params(("parallel", "parallel", "arbitrary")))(*args)


def _log_sigmoid(x):
    return jnp.minimum(x, 0.0) - jnp.log(1.0 + jnp.exp(-jnp.abs(x)))


def _mlstm_kernel(bias_ref, q_ref, k_ref, v_ref, og_ref, gr_ref, gc_ref, c0_ref, n0_ref, m0_ref, g_ref,
                  y_ref, co_ref, no_ref, mo_ref, hf_ref, hb_ref, *, nc, hp):
    L = MLSTM_CHUNK
    ri = lax.broadcasted_iota(I32, (L, L), 0)
    ci = lax.broadcasted_iota(I32, (L, L), 1)

    def chunk(d, a, c, state):
        c_prev, n_prev, m_prev = state
        head = pl.program_id(1) * hp + a
        cols = slice(a * LANE, (a + 1) * LANE)
        rows = pl.ds(pl.multiple_of(c * L, L), L)
        qf = q_ref[rows, cols] * (MLSTM_DK ** -0.5)
        kf = k_ref[rows, cols]
        qb, kb, vb = qf.astype(BF16), kf.astype(BF16), v_ref[rows, cols].astype(BF16)
        gr = gr_ref[a, c]
        gc = gc_ref[a, c]
        bi = bias_ref[2 * d * MLSTM_HEADS + head]
        bf = bias_ref[(2 * d + 1) * MLSTM_HEADS + head]
        i_row = gr[2 * d:2 * d + 1, :] + bi
        f_row = _log_sigmoid(gr[2 * d + 1:2 * d + 2, :] + bf)
        i_col = gc[:, 2 * d:2 * d + 1] + bi
        f_col = _log_sigmoid(gc[:, 2 * d + 1:2 * d + 2] + bf)
        mask = (ci <= ri) if d == 0 else (ci >= ri)
        mask_t = (ri <= ci) if d == 0 else (ri >= ci)
        b_col = jnp.sum(jnp.where(mask, f_row, 0.0), axis=1, keepdims=True)
        b_row = jnp.sum(jnp.where(mask_t, f_col, 0.0), axis=0, keepdims=True)
        d_intra = jnp.where(mask, b_col - b_row + i_row, -jnp.inf)
        d_inter = b_col + m_prev
        m_row = jnp.maximum(d_inter, jnp.max(d_intra, axis=1, keepdims=True))
        w_intra = jnp.exp(d_intra - m_row)
        w_inter = jnp.exp(d_inter - m_row)
        s = lax.dot_general(qb, kb, _NT, preferred_element_type=F32) * w_intra
        num = (jnp.dot(s.astype(BF16), vb, preferred_element_type=F32)
               + w_inter * jnp.dot(qb, c_prev.astype(BF16), preferred_element_type=F32))
        den = jnp.sum(s, axis=1, keepdims=True) + w_inter * jnp.sum(qf * n_prev, axis=1, keepdims=True)
        hh = num / jnp.maximum(jnp.abs(den), jnp.exp(-m_row))
        last = L - 1 if d == 0 else 0
        m_new = m_row[last:last + 1, :]
        b_last = b_col[last:last + 1, :]
        w_state = jnp.exp(b_last - b_col + i_col - m_new)
        decay = jnp.exp(b_last + m_prev - m_new)
        kw = w_state * kf
        c_new = decay * c_prev + lax.dot_general(kw.astype(BF16), vb, (((0,), (0,)), ((), ())),
                                                 preferred_element_type=F32)
        n_new = decay * n_prev + jnp.sum(kw, axis=0, keepdims=True)
        if d == 0:
            hf_ref[rows, cols] = hh
        else:
            hb_ref[rows, cols] = hh
        return c_new, n_new, m_new

    chains = [(d, a) for a in range(hp) for d in (0, 1)]

    def body(j, states):
        return tuple(chunk(d, a, j if d == 0 else nc - 1 - j, st) for (d, a), st in zip(chains, states))

    init = tuple((c0_ref[d, a], n0_ref[d, a], m0_ref[d, a]) for d, a in chains)
    final = lax.fori_loop(0, nc, body, init)
    for (d, a), (c_fin, n_fin, m_fin) in zip(chains, final):
        co_ref[d, a] = c_fin
        no_ref[d, a] = n_fin
        mo_ref[d, a] = m_fin
    for a in range(hp):
        cols = slice(a * LANE, (a + 1) * LANE)
        hs = hf_ref[:, cols] + hb_ref[:, cols]
        y = hs * lax.rsqrt(jnp.mean(hs * hs, axis=-1, keepdims=True) + NORM_EPS) * g_ref[a]
        y_ref[:, cols] = (y * jax.nn.sigmoid(og_ref[:, cols])).astype(y_ref.dtype)


def _mlstm(y_buf, proj, lay, gr, gc, bias, c0, n0, m0, gnorm, *, batch, seq, row0):
    nc = seq // MLSTM_CHUNK
    assert row0 % seq == 0
    rb0 = row0 // seq
    H = MLSTM_HEADS
    per_head = seq * LANE * 4 * (4 * 2 + 2 + 2)
    hp = max(h for h in (1, 2, 4) if h == 1 or h * per_head <= MLSTM_VMEM_BUDGET)
    wide = hp * LANE

    def pspec(name):
        cb = lay[name][0] // wide
        return pl.BlockSpec((seq, wide), lambda b, h: (rb0 + b, cb + h))

    st = lambda a, b_: pl.BlockSpec((None, 2, hp, a, b_), lambda b, h: (b, 0, h, 0, 0))
    kern = functools.partial(_mlstm_kernel, nc=nc, hp=hp)

    def body(*refs):
        kern(*refs[:11], *refs[12:])

    return pl.pallas_call(
        body, grid=(batch, H // hp),
        in_specs=[pl.BlockSpec(memory_space=pltpu.SMEM),
                  pspec("mq"), pspec("mk"), pspec("mv"), pspec("mo"),
                  pl.BlockSpec((hp, nc, 4, MLSTM_CHUNK), lambda b, h: (h, rb0 + b, 0, 0)),
                  pl.BlockSpec((hp, nc, MLSTM_CHUNK, 4), lambda b, h: (h, rb0 + b, 0, 0)),
                  st(MLSTM_DK, MLSTM_DV), st(1, MLSTM_DK), st(1, 1),
                  pl.BlockSpec((hp, 1, MLSTM_DV), lambda b, h: (h, 0, 0)),
                  pl.BlockSpec(memory_space=pl.ANY)],
        out_specs=[pl.BlockSpec((seq, wide), lambda b, h: (rb0 + b, h)),
                   st(MLSTM_DK, MLSTM_DV), st(1, MLSTM_DK), st(1, 1)],
        out_shape=[jax.ShapeDtypeStruct(y_buf.shape, y_buf.dtype),
                   jax.ShapeDtypeStruct((batch, 2, H, MLSTM_DK, MLSTM_DV), F32),
                   jax.ShapeDtypeStruct((batch, 2, H, 1, MLSTM_DK), F32),
                   jax.ShapeDtypeStruct((batch, 2, H, 1, 1), F32)],
        scratch_shapes=[pltpu.VMEM((seq, wide), F32), pltpu.VMEM((seq, wide), F32)],
        input_output_aliases={11: 0},
        compiler_params=_params(("parallel", "arbitrary")))(
            bias, proj, proj, proj, proj, gr, gc, c0, n0, m0, gnorm, y_buf)


def _merge_kernel(yb_ref, ya_ref, yc_ref, m0_ref, m1_ref, m2_ref, wb_ref, wa_ref, wc_ref, o_ref):
    z = jax.nn.sigmoid(m0_ref[...]) * jnp.dot(ya_ref[...], wa_ref[...], preferred_element_type=F32)
    z = z + jax.nn.sigmoid(m1_ref[...]) * jnp.dot(yb_ref[...], wb_ref[...], preferred_element_type=F32)
    z = z + jax.nn.sigmoid(m2_ref[...]) * jnp.dot(yc_ref[...], wc_ref[...], preferred_element_type=F32)
    o_ref[...] = z.astype(o_ref.dtype)


def _merge(ya, yb, yc, proj, lay, wbr, d_model):
    n = ya.shape[0]
    tm, tn = _tile(n, 512), _tile(d_model, 512)
    moff = lay["merge"][0] // tn
    nj = d_model // tn
    gate = lambda br: pl.BlockSpec((tm, tn), lambda i, j: (i, moff + br * nj + j))
    return pl.pallas_call(
        _merge_kernel, grid=(n // tm, nj),
        in_specs=[pl.BlockSpec((tm, GQA_WIDTH), lambda i, j: (i, 0)), pl.BlockSpec((tm, MLA_WIDTH), lambda i, j: (i, 0)),
                  pl.BlockSpec((tm, MLSTM_WIDTH), lambda i, j: (i, 0)), gate(0), gate(1), gate(2),
                  pl.BlockSpec((GQA_WIDTH, tn), lambda i, j: (0, j)),
                  pl.BlockSpec((MLA_WIDTH, tn), lambda i, j: (GQA_WIDTH // MLA_WIDTH, j)),
                  pl.BlockSpec((MLSTM_WIDTH, tn), lambda i, j: ((GQA_WIDTH + MLA_WIDTH) // MLSTM_WIDTH, j))],
        out_specs=pl.BlockSpec((tm, tn), lambda i, j: (i, j)),
        out_shape=jax.ShapeDtypeStruct((n, d_model), BF16),
        compiler_params=_params(("parallel", "parallel")))(yb, ya, yc, proj, proj, proj, wbr, wbr, wbr)


def _mm_res_kernel(z_ref, w_ref, x_ref, g_ref, o_ref):
    o_ref[...] = x_ref[...] + g_ref[...] * jnp.dot(z_ref[...], w_ref[...], preferred_element_type=F32)


def _matmul_residual(z, w, x, mod3, g_blk, n_prompt, dec_seq):
    n, k = z.shape
    d = w.shape[1]
    tm = _tile(np.gcd(n_prompt, dec_seq), 1024)
    tn = _tile(d, 512)
    nj = d // tn
    mrow = functools.partial(_mod_row, tm=tm, n_prompt=n_prompt, dec_seq=dec_seq)
    return pl.pallas_call(
        _mm_res_kernel, grid=(n // tm, nj),
        in_specs=[pl.BlockSpec((tm, k), lambda i, j: (i, 0)), pl.BlockSpec((k, tn), lambda i, j: (0, j)),
                  pl.BlockSpec((tm, tn), lambda i, j: (i, j)),
                  pl.BlockSpec((None, 1, tn), lambda i, j: (mrow(i), 0, g_blk * nj + j))],
        out_specs=pl.BlockSpec((tm, tn), lambda i, j: (i, j)),
        out_shape=jax.ShapeDtypeStruct((n, d), F32),
        compiler_params=_params(("parallel", "parallel")))(z, w, x, mod3)


def _router_kernel(x_ref, g_ref, sc_ref, sh_ref, rw_ref, rb_ref, h_ref, ti_ref, tg_ref, rk_ref, cnt_ref, carry_ref):
    @pl.when(pl.program_id(0) == 0)
    def _():
        carry_ref[...] = jnp.zeros_like(carry_ref)

    x = x_ref[...]
    r = lax.rsqrt(jnp.mean(x * x, axis=-1, keepdims=True) + NORM_EPS)
    h = (x * r * g_ref[...]) * (1.0 + sc_ref[...]) + sh_ref[...]
    hh = h.astype(BF16)
    half = h.shape[1] // 2
    bits = lax.bitcast_convert_type(hh.astype(F32), jnp.uint32)
    h_ref[...] = (bits[:, :half] | (bits[:, half:] >> 16)).reshape(h.shape[0], 1, half)
    hl = (h - hh.astype(F32)).astype(BF16)
    rw = rw_ref[...]
    rh = rw.astype(BF16)
    rl = (rw - rh.astype(F32)).astype(BF16)
    lg = (lax.dot_general(rh, hh, _NT, preferred_element_type=F32)
          + lax.dot_general(rh, hl, _NT, preferred_element_type=F32)
          + lax.dot_general(rl, hh, _NT, preferred_element_type=F32)) + rb_ref[...]
    n_exp, tm = lg.shape
    eidx = lax.broadcasted_iota(I32, lg.shape, 0).astype(F32)
    vals, idxs = [], []
    cur = lg
    for _k in range(TOP_K):
        m = jnp.max(cur, axis=0, keepdims=True)
        ix = jnp.min(jnp.where(cur == m, eidx, float(n_exp)), axis=0, keepdims=True)
        vals.append(m)
        idxs.append(ix)
        cur = jnp.where(eidx == ix, -jnp.inf, cur)
    ex = [jnp.exp(v - vals[0]) for v in vals]
    den = ex[0] + ex[1] + ex[2] + ex[3]
    onehot = jnp.zeros(lg.shape, F32)
    for ix in idxs:
        onehot = onehot + (eidx == ix).astype(F32)
    ti = lax.broadcasted_iota(I32, (tm, tm), 0)
    tj = lax.broadcasted_iota(I32, (tm, tm), 1)
    tri = (ti <= tj).astype(BF16)
    cum = jnp.dot(onehot.astype(BF16), tri, preferred_element_type=F32)
    excl = cum - onehot + carry_ref[...]
    for k in range(TOP_K):
        ti_ref[k:k + 1, :] = idxs[k].astype(I32)
        tg_ref[k:k + 1, :] = ex[k] / den
        rk_ref[k:k + 1, :] = jnp.sum(jnp.where(eidx == idxs[k], excl, 0.0), axis=0, keepdims=True).astype(I32)
    carry_ref[...] = carry_ref[...] + jnp.sum(onehot, axis=1, keepdims=True)
    cnt_ref[...] = carry_ref[...]


def _router(x, g, mod3, sc_blk, sh_blk, rw_t, rb, n_prompt, dec_seq):
    n, d = x.shape
    n_exp = rw_t.shape[0]
    tm = _tile(np.gcd(n_prompt, dec_seq), 256)
    mrow = functools.partial(_mod_row, tm=tm, n_prompt=n_prompt, dec_seq=dec_seq)
    kspec = pl.BlockSpec((TOP_K, tm), lambda i: (0, i))
    return pl.pallas_call(
        _router_kernel, grid=(n // tm,),
        in_specs=[pl.BlockSpec((tm, d), lambda i: (i, 0)), pl.BlockSpec((1, d), lambda i: (0, 0)),
                  pl.BlockSpec((None, 1, d), lambda i: (mrow(i), 0, sc_blk)),
                  pl.BlockSpec((None, 1, d), lambda i: (mrow(i), 0, sh_blk)),
                  pl.BlockSpec((n_exp, d), lambda i: (0, 0)), pl.BlockSpec((n_exp, 1), lambda i: (0, 0))],
        out_specs=[pl.BlockSpec((tm, 1, d // 2), lambda i: (i, 0, 0)), kspec, kspec, kspec,
                   pl.BlockSpec((n_exp, 1), lambda i: (0, 0))],
        out_shape=[jax.ShapeDtypeStruct((n, 1, d // 2), jnp.uint32), jax.ShapeDtypeStruct((TOP_K, n), I32),
                   jax.ShapeDtypeStruct((TOP_K, n), F32), jax.ShapeDtypeStruct((TOP_K, n), I32),
                   jax.ShapeDtypeStruct((n_exp, 1), F32)],
        scratch_shapes=[pltpu.VMEM((n_exp, 1), F32)],
        compiler_params=_params(("arbitrary",)))(x, g.reshape(1, d), mod3, mod3, rw_t, rb.reshape(n_exp, 1))


def _start_row_gather(idx_ref, src_ref, buf_ref, sem, rows):
    def issue(r8, carry):
        for u in range(DMA_UNROLL):
            r = r8 * DMA_UNROLL + u
            pltpu.make_async_copy(src_ref.at[idx_ref[0, r]], buf_ref.at[r], sem).start()
        return carry

    lax.fori_loop(0, rows // DMA_UNROLL, issue, 0)


def _wait_row_gather(src_ref, buf_ref, sem, rows):
    pltpu.make_async_copy(src_ref.at[pl.ds(0, rows)], buf_ref, sem).wait()


def _ffn_up_kernel(be_ref, na_ref, idx0_ref, idx1_ref, hp_ref, wg_ref, wu_ref, bg_ref, bu_ref, a_ref,
                   xg0_ref, xg1_ref, x2d_ref, xbf_ref, sem):
    j, f = pl.program_id(0), pl.program_id(1)
    na = na_ref[0]
    bm, half = x2d_ref.shape

    @pl.when((f == 0) & (j < na))
    def _():
        @pl.when(j == 0)
        def _():
            _start_row_gather(idx0_ref, hp_ref, xg0_ref, sem.at[0], bm)

        for par in (0, 1):
            @pl.when(j % 2 == par)
            def _(par=par):
                cur, nxt = (xg0_ref, xg1_ref) if par == 0 else (xg1_ref, xg0_ref)
                _wait_row_gather(hp_ref, cur, sem.at[par], bm)

                @pl.when(j + 1 < na)
                def _():
                    _start_row_gather(idx1_ref, hp_ref, nxt, sem.at[1 - par], bm)

                x2d_ref[...] = cur[...].reshape(bm, half)
                bits = x2d_ref[...]
                xbf_ref[:, :half] = lax.bitcast_convert_type(bits & jnp.uint32(0xFFFF0000), F32).astype(BF16)
                xbf_ref[:, half:] = lax.bitcast_convert_type(bits << 16, F32).astype(BF16)

    @pl.when(j < na)
    def _():
        x = xbf_ref[...]
        g = jnp.dot(x, wg_ref[...], preferred_element_type=F32) + bg_ref[...]
        u = jnp.dot(x, wu_ref[...], preferred_element_type=F32) + bu_ref[...]
        g = jnp.minimum(g, SWIGLU_LIMIT)
        u = jnp.clip(u, -SWIGLU_LIMIT, SWIGLU_LIMIT)
        a_ref[...] = ((u + 1.0) * g * jax.nn.sigmoid(SWIGLU_ALPHA * g)).astype(BF16)

    @pl.when(j >= na)
    def _():
        a_ref[...] = jnp.zeros_like(a_ref)


def _ffn_down_kernel(be_ref, na_ref, a_ref, wd_ref, bd_ref, o_ref):
    active = pl.program_id(0) < na_ref[0]

    @pl.when(active)
    def _():
        y = jnp.dot(a_ref[...], wd_ref[...], preferred_element_type=F32) + bd_ref[...]
        o_ref[...] = y.reshape(o_ref.shape)

    @pl.when(jnp.logical_not(active))
    def _():
        o_ref[...] = jnp.zeros_like(o_ref)


def _expert_ffn(hp, slot_tok, blk_expert, n_active, wg, wu, wd, bg, bu, bd, layer):
    n_slots = slot_tok.shape[0]
    d = 2 * hp.shape[2]
    n_exp, _, d_ff = wg.shape[1:]
    bm = MOE_ROWS
    nb = n_slots // bm
    tf = _tile(d_ff, 512)
    nf = d_ff // tf
    idx = slot_tok.reshape(nb, 1, bm)

    def jj(j, na):
        return jnp.minimum(j, na[0] - 1)

    def ff(j, f, na, last):
        return jnp.where(j < na[0], f, last)

    up_spec = pltpu.PrefetchScalarGridSpec(
        num_scalar_prefetch=2, grid=(nb, nf),
        in_specs=[pl.BlockSpec((None, 1, bm), lambda j, f, be, na: (jj(j, na), 0, 0), memory_space=pltpu.SMEM),
                  pl.BlockSpec((None, 1, bm), lambda j, f, be, na: (jnp.minimum(jj(j, na) + 1, nb - 1), 0, 0),
                               memory_space=pltpu.SMEM),
                  pl.BlockSpec(memory_space=pl.ANY),
                  pl.BlockSpec((None, None, d, tf), lambda j, f, be, na: (layer, be[jj(j, na)], 0, ff(j, f, na, nf - 1))),
                  pl.BlockSpec((None, None, d, tf), lambda j, f, be, na: (layer, be[jj(j, na)], 0, ff(j, f, na, nf - 1))),
                  pl.BlockSpec((None, None, 1, tf), lambda j, f, be, na: (layer, be[jj(j, na)], 0, ff(j, f, na, nf - 1))),
                  pl.BlockSpec((None, None, 1, tf), lambda j, f, be, na: (layer, be[jj(j, na)], 0, ff(j, f, na, nf - 1)))],
        out_specs=pl.BlockSpec((bm, tf), lambda j, f, be, na: (j, f)),
        scratch_shapes=[pltpu.VMEM((bm, 1, d // 2), jnp.uint32), pltpu.VMEM((bm, 1, d // 2), jnp.uint32),
                        pltpu.VMEM((bm, d // 2), jnp.uint32), pltpu.VMEM((bm, d), BF16),
                        pltpu.SemaphoreType.DMA((2,))])
    act = pl.pallas_call(
        _ffn_up_kernel, grid_spec=up_spec,
        out_shape=jax.ShapeDtypeStruct((n_slots, d_ff), BF16),
        compiler_params=_params(("arbitrary", "arbitrary")))(blk_expert, n_active, idx, idx, hp, wg, wu, bg, bu)

    tn = _tile(d, 2048)
    nn = d // tn
    down_spec = pltpu.PrefetchScalarGridSpec(
        num_scalar_prefetch=2, grid=(nb, nn),
        in_specs=[pl.BlockSpec((bm, d_ff), lambda j, c, be, na: (jj(j, na), 0)),
                  pl.BlockSpec((None, None, d_ff, tn), lambda j, c, be, na: (layer, be[jj(j, na)], 0, ff(j, c, na, nn - 1))),
                  pl.BlockSpec((None, None, 1, tn), lambda j, c, be, na: (layer, be[jj(j, na)], 0, ff(j, c, na, nn - 1)))],
        out_specs=pl.BlockSpec((bm, 1, tn), lambda j, c, be, na: (j, 0, c)))
    return pl.pallas_call(
        _ffn_down_kernel, grid_spec=down_spec,
        out_shape=jax.ShapeDtypeStruct((n_slots, 1, d), F32),
        compiler_params=_params(("arbitrary", "arbitrary")))(blk_expert, n_active, act, wd, bd)


def _combine_kernel(idx0_ref, idx1_ref, yb_ref, x_ref, g_ref, w_ref, o_ref, g0_ref, g1_ref, y2d_ref, sem):
    i, n = pl.program_id(0), pl.num_programs(0)
    rows, d = y2d_ref.shape
    tm = rows // TOP_K

    @pl.when(i == 0)
    def _():
        _start_row_gather(idx0_ref, yb_ref, g0_ref, sem.at[0], rows)

    for par in (0, 1):
        @pl.when(i % 2 == par)
        def _(par=par):
            cur, nxt = (g0_ref, g1_ref) if par == 0 else (g1_ref, g0_ref)
            _wait_row_gather(yb_ref, cur, sem.at[par], rows)

            @pl.when(i + 1 < n)
            def _():
                _start_row_gather(idx1_ref, yb_ref, nxt, sem.at[1 - par], rows)

            y2d_ref[...] = cur[...].reshape(rows, d)
            w = w_ref[...]
            y = w[:, 0:1] * y2d_ref[0:tm, :]
            for k in range(1, TOP_K):
                y = y + w[:, k:k + 1] * y2d_ref[k * tm:(k + 1) * tm, :]
            o_ref[...] = x_ref[...] + g_ref[...] * y


def _combine(x, yb, dest, gate, mod3, g_blk, n_prompt, dec_seq):
    n, d = x.shape
    tm = _tile(np.gcd(n_prompt, dec_seq), 128)
    nt = n // tm
    rows = TOP_K * tm
    idx = dest.reshape(TOP_K, nt, tm).transpose(1, 0, 2).reshape(nt, 1, rows)
    mrow = functools.partial(_mod_row, tm=tm, n_prompt=n_prompt, dec_seq=dec_seq)
    return pl.pallas_call(
        _combine_kernel, grid=(nt,),
        in_specs=[pl.BlockSpec((None, 1, rows), lambda i: (i, 0, 0), memory_space=pltpu.SMEM),
                  pl.BlockSpec((None, 1, rows), lambda i: (jnp.minimum(i + 1, nt - 1), 0, 0), memory_space=pltpu.SMEM),
                  pl.BlockSpec(memory_space=pl.ANY),
                  pl.BlockSpec((tm, d), lambda i: (i, 0)),
                  pl.BlockSpec((None, 1, d), lambda i: (mrow(i), 0, g_blk)),
                  pl.BlockSpec((tm, TOP_K), lambda i: (i, 0))],
        out_specs=pl.BlockSpec((tm, d), lambda i: (i, 0)),
        out_shape=jax.ShapeDtypeStruct((n, d), F32),
        scratch_shapes=[pltpu.VMEM((rows, 1, d), F32), pltpu.VMEM((rows, 1, d), F32), pltpu.VMEM((rows, d), F32),
                        pltpu.SemaphoreType.DMA((2,))],
        compiler_params=_params(("arbitrary",)))(idx, idx, yb, x, mod3, gate.T)


def _moe(x, p, layer, mod3, n_prompt, dec_seq):
    n, d = x.shape
    n_exp = p["router_w"].shape[-1]
    bm = MOE_ROWS
    h, topi, gate, rank, cnt = _router(x, p["norm_ffn"][layer], mod3, 4, 3, p["router_w"][layer].T,
                                       p["router_b"][layer], n_prompt, dec_seq)
    counts = cnt[:, 0].astype(I32)
    padded = (counts + bm - 1) // bm * bm
    pad_end = jnp.cumsum(padded)
    pad_start = pad_end - padded
    first = jnp.sum(jnp.where(topi[None] == jnp.arange(n_exp, dtype=I32)[:, None, None], pad_start[:, None, None], 0),
                    axis=0)
    dest = first + rank
    n_slots = n * TOP_K + n_exp * bm
    nb = n_slots // bm
    slot_tok = jnp.zeros((n_slots,), I32).at[dest.reshape(-1)].set(jnp.tile(jnp.arange(n, dtype=I32), TOP_K))
    blk_expert = jnp.minimum(jnp.searchsorted(pad_end, jnp.arange(nb, dtype=I32) * bm, side="right"),
                             n_exp - 1).astype(I32)
    n_active = (pad_end[-1:] // bm).astype(I32)
    yb = _expert_ffn(h, slot_tok, blk_expert, n_active, p["moe_w_gate"], p["moe_w_up"], p["moe_w_down"],
                     p["moe_b_gate"], p["moe_b_up"], p["moe_b_down"], layer)
    return _combine(x, yb, dest, gate, mod3, 5, n_prompt, dec_seq)


def kernel(x_prompt, x_sample, cache_mla_ckv, cache_mla_krope, cache_gqa_k, cache_gqa_v, state_mlstm_C, state_mlstm_n, state_mlstm_m, c, c_ctx, w_mod, b_mod, norm_attn, norm_ffn, w_in, mla_q_norm, mla_w_qb, mla_kv_norm, mla_w_kvb, gqa_q_norm, gqa_k_norm, mlstm_gate_bias, mlstm_out_norm, w_branch, w_out, router_w, router_b, moe_w_gate, moe_b_gate, moe_w_up, moe_b_up, moe_w_down, moe_b_down, final_norm):
    bp, sp, d = x_prompt.shape
    bs, ss, _ = x_sample.shape
    depth = w_mod.shape[0]
    past = cache_mla_ckv.shape[2]
    n_p, n_s = bp * sp, bs * ss
    n = n_p + n_s
    tk_s = ss + past
    H = MLSTM_HEADS
    lay, _ = _proj_layout(d)
    n_exp = router_w.shape[-1]

    x = jnp.concatenate([x_prompt.reshape(n_p, d), x_sample.reshape(n_s, d)], axis=0)
    cond = jnp.zeros((16, d), F32).at[0].set(c_ctx).at[1:1 + bs].set(c)
    tabs = _rope_tables(n_p, bs, ss, MLA_ROPE) + _rope_tables(n_p, bs, ss, GQA_HEAD_DIM)
    moe = dict(router_w=router_w, router_b=router_b, norm_ffn=norm_ffn,
               moe_w_gate=moe_w_gate.astype(BF16), moe_w_up=moe_w_up.astype(BF16), moe_w_down=moe_w_down.astype(BF16),
               moe_b_gate=moe_b_gate.reshape(depth, n_exp, 1, -1), moe_b_up=moe_b_up.reshape(depth, n_exp, 1, -1),
               moe_b_down=moe_b_down.reshape(depth, n_exp, 1, d))
    zc = jnp.zeros((bp, 2, H, MLSTM_DK, MLSTM_DV), F32)
    zn = jnp.zeros((bp, 2, H, 1, MLSTM_DK), F32)
    zm = jnp.zeros((bp, 2, H, 1, 1), F32)
    ctx = []
    for l in range(depth):
        mod3 = _modulation(cond, w_mod[l], b_mod[l]).reshape(16, 1, 6 * d)
        h = _norm_mod(x, norm_attn[l], mod3, 1, 0, n_p, ss)
        proj = _matmul(h, _build_w_in(w_in[l], d), F32, 1024, 512)
        qn = jnp.pad(mla_q_norm[l], (0, Q_LORA_PAD - MLA_Q_LORA)).reshape(1, Q_LORA_PAD)
        qm, ckv32, ckv16, kr16, qg, kg32, kg16, v16 = _prep(
            proj, lay, tabs, qn, mla_kv_norm[l].reshape(1, -1), gqa_q_norm[l].reshape(1, -1),
            gqa_k_norm[l].reshape(1, -1), _build_w_qb(mla_w_qb[l]))
        ckv_all = jnp.concatenate([
            jnp.concatenate([ckv16[n_p:].reshape(bs, ss, -1), cache_mla_ckv[:, l].astype(BF16)], axis=1).reshape(bs * tk_s, -1),
            ckv16[:n_p]], axis=0)
        kv = _matmul(ckv_all, _build_w_kvb(mla_w_kvb[l]), BF16, 1024, 512)
        kr_s = jnp.concatenate([kr16[n_p:].reshape(bs, ss, LANE),
                                jnp.pad(cache_mla_krope[:, l], ((0, 0), (0, 0), (0, LANE - MLA_ROPE))).astype(BF16)],
                               axis=1).reshape(bs * tk_s, LANE)
        kg_s = jnp.concatenate([kg16[n_p:].reshape(bs, ss, -1), cache_gqa_k[:, l].reshape(bs, past, -1).astype(BF16)],
                               axis=1).reshape(bs * tk_s, -1)
        vg_s = jnp.concatenate([v16[n_p:].reshape(bs, ss, -1), cache_gqa_v[:, l].reshape(bs, past, -1).astype(BF16)],
                               axis=1).reshape(bs * tk_s, -1)
        ya = jnp.zeros((n, MLA_WIDTH), BF16)
        mla = dict(heads=MLA_HEADS, shared_kv=False, q_col0=0, k_col0=0, v_col0=MLA_HEADS, q2_col0=MLA_HEADS)
        ya = _attention_call(ya, qm, kv, kv, batch=bp, tq_len=sp, tk_len=sp, hp=MLA_HEADS, tq_cap=256, q_row0=0,
                             k_row0=bs * tk_s, k2=kr16, **mla)
        ya = _attention_call(ya, qm, kv, kv, batch=bs, tq_len=ss, tk_len=tk_s, hp=1, tq_cap=512, q_row0=n_p,
                             k_row0=0, k2=kr_s, **mla)
        gqa = dict(heads=GQA_HEADS, hp=GQA_HEADS // GQA_KV_HEADS, tq_cap=256, shared_kv=True, q_col0=0, k_col0=0,
                   v_col0=0)
        yb = jnp.zeros((n, GQA_WIDTH), BF16)
        yb = _attention_call(yb, qg, kg16, v16, batch=bp, tq_len=sp, tk_len=sp, q_row0=0, k_row0=0, **gqa)
        yb = _attention_call(yb, qg, kg_s, vg_s, batch=bs, tq_len=ss, tk_len=tk_s, q_row0=n_p, k_row0=0, **gqa)
        mg0 = lay["mg"][0]
        mg = proj[:, mg0:mg0 + 4 * H].reshape(n // MLSTM_CHUNK, MLSTM_CHUNK, 4, H)
        gr = mg.transpose(3, 0, 2, 1)
        gc = mg.transpose(3, 0, 1, 2)
        bias = mlstm_gate_bias[l].reshape(-1)
        gnorm = mlstm_out_norm[l].reshape(H, 1, MLSTM_DV)
        yc = jnp.zeros((n, MLSTM_WIDTH), BF16)
        yc, c_new, n_new, m_new = _mlstm(yc, proj, lay, gr, gc, bias, zc, zn, zm, gnorm, batch=bp, seq=sp, row0=0)
        yc, _, _, _ = _mlstm(yc, proj, lay, gr, gc, bias, state_mlstm_C[:, l],
                             state_mlstm_n[:, l].reshape(bs, 2, H, 1, MLSTM_DK),
                             state_mlstm_m[:, l].reshape(bs, 2, H, 1, 1), gnorm, batch=bs, seq=ss, row0=n_p)
        wb = w_branch[l].astype(BF16)
        wbr = jnp.concatenate([wb[MLA_WIDTH:MLA_WIDTH + GQA_WIDTH], wb[:MLA_WIDTH], wb[MLA_WIDTH + GQA_WIDTH:]], axis=0)
        z = _merge(ya, yb, yc, proj, lay, wbr, d)
        x = _matmul_residual(z, w_out[l].astype(BF16), x, mod3, 2, n_p, ss)
        x = _moe(x, moe, l, mod3, n_p, ss)
        kr0, gv0 = lay["kr"][0], lay["gv"][0]
        ctx.append((ckv32[:n_p].reshape(bp, sp, -1), proj[:n_p, kr0:kr0 + MLA_ROPE].reshape(bp, sp, MLA_ROPE),
                    kg32[:n_p].reshape(bp, sp, GQA_KV_HEADS, GQA_HEAD_DIM),
                    proj[:n_p, gv0:gv0 + GQA_KV_WIDTH].reshape(bp, sp, GQA_KV_HEADS, GQA_HEAD_DIM),
                    c_new, n_new.reshape(bp, 2, H, MLSTM_DK), m_new.reshape(bp, 2, H)))
    y = _rmsnorm(x, final_norm)
    outs = [jnp.stack([t[i] for t in ctx], axis=1) for i in range(7)]
    return (y[:n_p].reshape(bp, sp, d), y[n_p:].reshape(bs, ss, d), *outs)
```

```python
import functools

import jax
import jax.numpy as jnp
import numpy as np
from jax import lax
from jax.experimental import pallas as pl
from jax.experimental.pallas import tpu as pltpu

F32 = jnp.float32
BF16 = jnp.bfloat16
I32 = jnp.int32

GRID_W = 64
ROPE_BASE = 10000.0
NORM_EPS = 1e-6

MLA_HEADS = 8
MLA_Q_LORA = 896
MLA_KV_LORA = 512
MLA_NOPE = 128
MLA_ROPE = 64
MLA_V = 128
GQA_HEADS = 16
GQA_KV_HEADS = 4
GQA_HEAD_DIM = 128
MLSTM_HEADS = 8
MLSTM_DK = 128
MLSTM_DV = 128
MLSTM_CHUNK = 64
N_BRANCHES = 3
TOP_K = 4
SWIGLU_LIMIT = 7.0
SWIGLU_ALPHA = 1.702

LANE = 128
VMEM_LIMIT = 56 * 1024 * 1024
MOE_ROWS = 512
DMA_UNROLL = 8
MLSTM_VMEM_BUDGET = 32 * 1024 * 1024

MLA_WIDTH = MLA_HEADS * MLA_V
GQA_WIDTH = GQA_HEADS * GQA_HEAD_DIM
GQA_KV_WIDTH = GQA_KV_HEADS * GQA_HEAD_DIM
MLSTM_WIDTH = MLSTM_HEADS * MLSTM_DV
Q_LORA_PAD = 1024
MLA_QK = 2 * LANE


def _tile(n, cap):
    for t in (2048, 1024, 512, 256, 128, 64, 32, 16, 8):
        if t <= cap and n % t == 0:
            return t
    raise ValueError(f"no tile for {n}")


def _params(sem, vmem=VMEM_LIMIT):
    return pltpu.CompilerParams(dimension_semantics=sem, vmem_limit_bytes=vmem)


def _proj_layout(d_model):
    segs = [("merge", N_BRANCHES * d_model, N_BRANCHES * d_model), ("gq", GQA_WIDTH, GQA_WIDTH),
            ("mq", MLSTM_WIDTH, MLSTM_WIDTH), ("mk", MLSTM_WIDTH, MLSTM_WIDTH), ("mv", MLSTM_WIDTH, MLSTM_WIDTH),
            ("mo", MLSTM_WIDTH, MLSTM_WIDTH), ("qa", MLA_Q_LORA, Q_LORA_PAD), ("kva", MLA_KV_LORA, MLA_KV_LORA),
            ("gk", GQA_KV_WIDTH, GQA_KV_WIDTH), ("gv", GQA_KV_WIDTH, GQA_KV_WIDTH),
            ("kr", MLA_ROPE, LANE), ("mg", 4 * MLSTM_HEADS, LANE)]
    off, out = 0, {}
    for name, w, wp in segs:
        assert off % wp == 0, (name, off, wp)
        out[name] = (off, w, wp)
        off += wp
    total = -(-off // 512) * 512
    return out, total


def _build_w_in(w_in_l, d_model):
    lay, total = _proj_layout(d_model)
    ref_sizes = dict(qa=MLA_Q_LORA, kva=MLA_KV_LORA, kr=MLA_ROPE, gq=GQA_WIDTH, gk=GQA_KV_WIDTH, gv=GQA_KV_WIDTH,
                     mq=MLSTM_WIDTH, mk=MLSTM_WIDTH, mv=MLSTM_WIDTH, mo=MLSTM_WIDTH, mg=4 * MLSTM_HEADS,
                     merge=N_BRANCHES * d_model)
    ref_order = ["qa", "kva", "kr", "gq", "gk", "gv", "mq", "mk", "mv", "mo", "mg", "merge"]
    src, o = {}, 0
    for name in ref_order:
        src[name] = (o, ref_sizes[name])
        o += ref_sizes[name]
    wb = w_in_l.astype(BF16)
    pieces, used = [], 0
    for name, (off, w, wp) in lay.items():
        s0, sw = src[name]
        pieces.append(wb[:, s0:s0 + sw])
        if wp > w:
            pieces.append(jnp.zeros((wb.shape[0], wp - w), BF16))
        used = off + wp
    if total > used:
        pieces.append(jnp.zeros((wb.shape[0], total - used), BF16))
    return jnp.concatenate(pieces, axis=1)


def _build_w_qb(w_qb_l):
    w = w_qb_l.astype(BF16).reshape(MLA_Q_LORA, MLA_HEADS, MLA_NOPE + MLA_ROPE)
    w = jnp.pad(w, ((0, Q_LORA_PAD - MLA_Q_LORA), (0, 0), (0, MLA_QK - MLA_NOPE - MLA_ROPE)))
    return w.reshape(Q_LORA_PAD, MLA_HEADS * MLA_QK)


def _build_w_kvb(w_kvb_l):
    w = w_kvb_l.astype(BF16).reshape(MLA_KV_LORA, MLA_HEADS, MLA_NOPE + MLA_V)
    return jnp.concatenate([w[:, :, :MLA_NOPE].reshape(MLA_KV_LORA, -1), w[:, :, MLA_NOPE:].reshape(MLA_KV_LORA, -1)],
                           axis=1)


def _rope_tables(n_prompt, dec_batch, dec_seq, rot_dim):
    quarter = rot_dim // 4
    grid_rows = dec_seq // GRID_W
    row = jnp.repeat(jnp.arange(grid_rows, dtype=F32), GRID_W)
    col = jnp.tile(jnp.arange(GRID_W, dtype=F32), grid_rows)
    inv_freq = ROPE_BASE ** (-jnp.arange(quarter, dtype=F32) / quarter)
    ang_r = row[:, None] * inv_freq
    ang_c = col[:, None] * inv_freq
    ang = jnp.concatenate([ang_r, ang_r, ang_c, ang_c], axis=-1)
    cos = jnp.pad(jnp.cos(ang), ((0, 0), (0, LANE - rot_dim)), constant_values=1.0)
    sin = jnp.pad(jnp.sin(ang), ((0, 0), (0, LANE - rot_dim)))
    cos = jnp.concatenate([jnp.ones((n_prompt, LANE), F32), jnp.tile(cos, (dec_batch, 1))], axis=0)
    sin = jnp.concatenate([jnp.zeros((n_prompt, LANE), F32), jnp.tile(sin, (dec_batch, 1))], axis=0)
    return cos, sin


def _mod_kernel(c_ref, w_ref, b_ref, o_ref):
    c = c_ref[...]
    a = (c * jax.nn.sigmoid(c)).astype(BF16)
    o_ref[...] = jnp.dot(a, w_ref[...].astype(BF16), preferred_element_type=F32) + b_ref[...]


def _modulation(cond, w_mod_l, b_mod_l):
    rows, d = cond.shape
    n = w_mod_l.shape[1]
    tn = _tile(n, 512)
    return pl.pallas_call(
        _mod_kernel, grid=(n // tn,),
        in_specs=[pl.BlockSpec((rows, d), lambda j: (0, 0)), pl.BlockSpec((d, tn), lambda j: (0, j)),
                  pl.BlockSpec((1, tn), lambda j: (0, j))],
        out_specs=pl.BlockSpec((rows, tn), lambda j: (0, j)),
        out_shape=jax.ShapeDtypeStruct((rows, n), F32),
        compiler_params=_params(("parallel",)))(cond, w_mod_l, b_mod_l.reshape(1, n))


def _mm_kernel(x_ref, w_ref, o_ref):
    o_ref[...] = jnp.dot(x_ref[...], w_ref[...], preferred_element_type=F32).astype(o_ref.dtype)


def _matmul(x, w, out_dtype, tm_cap, tn_cap):
    m, k = x.shape
    n = w.shape[1]
    tm, tn = _tile(m, tm_cap), _tile(n, tn_cap)
    return pl.pallas_call(
        _mm_kernel, grid=(m // tm, n // tn),
        in_specs=[pl.BlockSpec((tm, k), lambda i, j: (i, 0)), pl.BlockSpec((k, tn), lambda i, j: (0, j))],
        out_specs=pl.BlockSpec((tm, tn), lambda i, j: (i, j)),
        out_shape=jax.ShapeDtypeStruct((m, n), out_dtype),
        compiler_params=_params(("parallel", "parallel")))(x, w)


def _mod_row(i, tm, n_prompt, dec_seq):
    r0 = i * tm
    return jnp.where(r0 < n_prompt, 0, 1 + (r0 - n_prompt) // dec_seq)


def _norm_mod_kernel(x_ref, g_ref, sc_ref, sh_ref, o_ref):
    x = x_ref[...]
    r = lax.rsqrt(jnp.mean(x * x, axis=-1, keepdims=True) + NORM_EPS)
    y = x * r * g_ref[...]
    o_ref[...] = (y * (1.0 + sc_ref[...]) + sh_ref[...]).astype(o_ref.dtype)


def _norm_mod(x, g, mod3, sc_blk, sh_blk, n_prompt, dec_seq):
    n, d = x.shape
    tm = _tile(np.gcd(n_prompt, dec_seq), 256)
    mrow = functools.partial(_mod_row, tm=tm, n_prompt=n_prompt, dec_seq=dec_seq)
    return pl.pallas_call(
        _norm_mod_kernel, grid=(n // tm,),
        in_specs=[pl.BlockSpec((tm, d), lambda i: (i, 0)), pl.BlockSpec((1, d), lambda i: (0, 0)),
                  pl.BlockSpec((None, 1, d), lambda i: (mrow(i), 0, sc_blk)),
                  pl.BlockSpec((None, 1, d), lambda i: (mrow(i), 0, sh_blk))],
        out_specs=pl.BlockSpec((tm, d), lambda i: (i, 0)),
        out_shape=jax.ShapeDtypeStruct((n, d), BF16),
        compiler_params=_params(("parallel",)))(x, g.reshape(1, d), mod3, mod3)


def _rms_kernel(x_ref, g_ref, o_ref):
    x = x_ref[...]
    r = lax.rsqrt(jnp.mean(x * x, axis=-1, keepdims=True) + NORM_EPS)
    o_ref[...] = x * r * g_ref[...]


def _rmsnorm(x, g, row0, rows):
    d = x.shape[1]
    tm = _tile(np.gcd(row0, rows) if row0 else rows, 256)
    b0 = row0 // tm
    return pl.pallas_call(
        _rms_kernel, grid=(rows // tm,),
        in_specs=[pl.BlockSpec((tm, d), lambda i: (b0 + i, 0)), pl.BlockSpec((1, d), lambda i: (0, 0))],
        out_specs=pl.BlockSpec((tm, d), lambda i: (i, 0)),
        out_shape=jax.ShapeDtypeStruct((rows, d), F32),
        compiler_params=_params(("parallel",)))(x, g.reshape(1, d))


def _rot_half(x, half):
    lane = lax.broadcasted_iota(I32, x.shape, 1)
    first = (lane % (2 * half)) < half
    return jnp.where(first, -pltpu.roll(x, LANE - half, 1), pltpu.roll(x, half, 1))


def _prep_kernel(qa_ref, kva_ref, gq_ref, gk_ref, gv_ref, kr_ref, ca_ref, sa_ref, cb_ref, sb_ref,
                 qn_ref, kvn_ref, gqn_ref, gkn_ref, wqb_ref,
                 qm_ref, ckv32_ref, ckv16_ref, kr16_ref, qg_ref, kg32_ref, kg16_ref, v16_ref):
    ca, sa, cb, sb = ca_ref[...], sa_ref[...], cb_ref[...], sb_ref[...]
    qa = qa_ref[...]
    r = lax.rsqrt(jnp.sum(qa * qa, axis=-1, keepdims=True) * (1.0 / MLA_Q_LORA) + NORM_EPS)
    qn = (qa * r * qn_ref[...]).astype(BF16)
    q = jnp.dot(qn, wqb_ref[...], preferred_element_type=F32)
    scale_a = (MLA_NOPE + MLA_ROPE) ** -0.5
    for h in range(MLA_HEADS):
        c0 = h * MLA_QK
        qm_ref[:, c0:c0 + LANE] = (q[:, c0:c0 + LANE] * scale_a).astype(BF16)
        x = q[:, c0 + LANE:c0 + 2 * LANE]
        qm_ref[:, c0 + LANE:c0 + 2 * LANE] = ((x * ca + _rot_half(x, MLA_ROPE // 4) * sa) * scale_a).astype(BF16)
    kva = kva_ref[...]
    r = lax.rsqrt(jnp.mean(kva * kva, axis=-1, keepdims=True) + NORM_EPS)
    ckv = kva * r * kvn_ref[...]
    ckv32_ref[...] = ckv
    ckv16_ref[...] = ckv.astype(BF16)
    kr = kr_ref[...]
    kr16_ref[...] = (kr * ca + _rot_half(kr, MLA_ROPE // 4) * sa).astype(BF16)
    scale_b = GQA_HEAD_DIM ** -0.5
    gqn, gkn = gqn_ref[...], gkn_ref[...]
    for h in range(GQA_HEADS):
        sl = slice(h * LANE, (h + 1) * LANE)
        x = gq_ref[:, sl]
        x = x * lax.rsqrt(jnp.mean(x * x, axis=-1, keepdims=True) + NORM_EPS) * gqn
        qg_ref[:, sl] = ((x * cb + _rot_half(x, GQA_HEAD_DIM // 4) * sb) * scale_b).astype(BF16)
    for h in range(GQA_KV_HEADS):
        sl = slice(h * LANE, (h + 1) * LANE)
        x = gk_ref[:, sl]
        x = x * lax.rsqrt(jnp.mean(x * x, axis=-1, keepdims=True) + NORM_EPS) * gkn
        kg32_ref[:, sl] = x
        kg16_ref[:, sl] = (x * cb + _rot_half(x, GQA_HEAD_DIM // 4) * sb).astype(BF16)
    v16_ref[...] = gv_ref[...].astype(BF16)


def _prep(proj, lay, tabs, qn, kvn, gqn, gkn, wqb):
    n = proj.shape[0]
    tm = _tile(n, 256)
    ca, sa, cb, sb = tabs

    def pspec(name):
        off, _, wp = lay[name]
        return pl.BlockSpec((tm, wp), lambda i: (i, off // wp))

    def row(w):
        return pl.BlockSpec((tm, w), lambda i: (i, 0))

    def full(a):
        return pl.BlockSpec(a.shape, lambda i: (0,) * a.ndim)

    outs = [(2 * MLA_HEADS * LANE, BF16), (MLA_KV_LORA, F32), (MLA_KV_LORA, BF16), (LANE, BF16),
            (GQA_WIDTH, BF16), (GQA_KV_WIDTH, F32), (GQA_KV_WIDTH, BF16), (GQA_KV_WIDTH, BF16)]
    return pl.pallas_call(
        _prep_kernel, grid=(n // tm,),
        in_specs=[pspec("qa"), pspec("kva"), pspec("gq"), pspec("gk"), pspec("gv"), pspec("kr"),
                  row(LANE), row(LANE), row(LANE), row(LANE), full(qn), full(kvn), full(gqn), full(gkn), full(wqb)],
        out_specs=[row(w) for w, _ in outs],
        out_shape=[jax.ShapeDtypeStruct((n, w), dt) for w, dt in outs],
        compiler_params=_params(("parallel",)))(proj, proj, proj, proj, proj, proj, ca, sa, cb, sb,
                                                 qn, kvn, gqn, gkn, wqb)


def _kv_expand_kernel(ckv_ref, kr_ref, wk_ref, wv_ref, k_ref, v_ref):
    ckv = ckv_ref[...]
    k = jnp.dot(ckv, wk_ref[...], preferred_element_type=F32).astype(BF16)
    kr = kr_ref[...]
    for h in range(MLA_HEADS):
        k_ref[:, h * MLA_QK:h * MLA_QK + LANE] = k[:, h * LANE:(h + 1) * LANE]
        k_ref[:, h * MLA_QK + LANE:(h + 1) * MLA_QK] = kr
    v_ref[...] = jnp.dot(ckv, wv_ref[...], preferred_element_type=F32).astype(BF16)


def _kv_expand(ckv, kr, w_kvb):
    rows, lora = ckv.shape
    tm = _tile(rows, 512)
    kw, vw = MLA_HEADS * MLA_NOPE, MLA_HEADS * MLA_V
    assert kw == vw
    return pl.pallas_call(
        _kv_expand_kernel, grid=(rows // tm,),
        in_specs=[pl.BlockSpec((tm, lora), lambda i: (i, 0)), pl.BlockSpec((tm, LANE), lambda i: (i, 0)),
                  pl.BlockSpec((lora, kw), lambda i: (0, 0)), pl.BlockSpec((lora, vw), lambda i: (0, 1))],
        out_specs=[pl.BlockSpec((tm, MLA_HEADS * MLA_QK), lambda i: (i, 0)), pl.BlockSpec((tm, vw), lambda i: (i, 0))],
        out_shape=[jax.ShapeDtypeStruct((rows, MLA_HEADS * MLA_QK), BF16), jax.ShapeDtypeStruct((rows, vw), BF16)],
        compiler_params=_params(("parallel",)))(ckv, kr, w_kvb, w_kvb)


_NT = (((1,), (1,)), ((), ()))


def _attn_kernel(q_ref, k_ref, v_ref, _, o_ref, *, hp, dq, shared_kv):
    for a in range(hp):
        kv = 0 if shared_kv else a
        s = lax.dot_general(q_ref[:, a * dq:(a + 1) * dq], k_ref[:, kv * dq:(kv + 1) * dq], _NT,
                            preferred_element_type=F32)
        m = jnp.max(s, axis=-1, keepdims=True)
        p = jnp.exp(s - m)
        l = jnp.sum(p, axis=-1, keepdims=True)
        o = jnp.dot(p.astype(BF16), v_ref[:, kv * LANE:(kv + 1) * LANE], preferred_element_type=F32)
        o_ref[:, a * LANE:(a + 1) * LANE] = (o / l).astype(o_ref.dtype)


def _attention_call(out_buf, q, k, v, *, batch, tq_len, tk_len, heads, hp, dq, tq_cap, shared_kv, q_row0, k_row0):
    tq = _tile(tq_len, tq_cap)
    nq = tq_len // tq
    assert q_row0 % tq == 0 and k_row0 % tk_len == 0 and heads % hp == 0
    qb0, kb0 = q_row0 // tq, k_row0 // tk_len
    kvh = 1 if shared_kv else hp
    return pl.pallas_call(
        functools.partial(_attn_kernel, hp=hp, dq=dq, shared_kv=shared_kv),
        grid=(batch, heads // hp, nq),
        in_specs=[pl.BlockSpec((tq, hp * dq), lambda b, g, i: (qb0 + b * nq + i, g)),
                  pl.BlockSpec((tk_len, kvh * dq), lambda b, g, i: (kb0 + b, g)),
                  pl.BlockSpec((tk_len, kvh * LANE), lambda b, g, i: (kb0 + b, g)),
                  pl.BlockSpec(memory_space=pl.ANY)],
        out_specs=pl.BlockSpec((tq, hp * LANE), lambda b, g, i: (qb0 + b * nq + i, g)),
        out_shape=jax.ShapeDtypeStruct(out_buf.shape, out_buf.dtype),
        input_output_aliases={3: 0},
        compiler_params=_params(("parallel", "parallel", "arbitrary")))(q, k, v, out_buf)


def _log_sigmoid(x):
    return jnp.minimum(x, 0.0) - jnp.log(1.0 + jnp.exp(-jnp.abs(x)))


def _mlstm_kernel(bias_ref, q_ref, k_ref, v_ref, og_ref, gr_ref, gc_ref, c0_ref, n0_ref, m0_ref, g_ref,
                  y_ref, co_ref, no_ref, mo_ref, hf_ref, hb_ref, *, nc, hp):
    L = MLSTM_CHUNK
    ri = lax.broadcasted_iota(I32, (L, L), 0)
    ci = lax.broadcasted_iota(I32, (L, L), 1)

    def gates(d, a, c, state):
        _, _, m_prev = state
        head = pl.program_id(1) * hp + a
        cols = slice(a * LANE, (a + 1) * LANE)
        rows = pl.ds(pl.multiple_of(c * L, L), L)
        qf = q_ref[rows, cols] * (MLSTM_DK ** -0.5)
        kf = k_ref[rows, cols]
        qb, kb, vb = qf.astype(BF16), kf.astype(BF16), v_ref[rows, cols].astype(BF16)
        gr = gr_ref[a, c]
        gc = gc_ref[a, c]
        bi = bias_ref[2 * d * MLSTM_HEADS + head]
        bf = bias_ref[(2 * d + 1) * MLSTM_HEADS + head]
        i_row = gr[2 * d:2 * d + 1, :] + bi
        f_row = _log_sigmoid(gr[2 * d + 1:2 * d + 2, :] + bf)
        i_col = gc[:, 2 * d:2 * d + 1] + bi
        f_col = _log_sigmoid(gc[:, 2 * d + 1:2 * d + 2] + bf)
        mask = (ci <= ri) if d == 0 else (ci >= ri)
        mask_t = (ri <= ci) if d == 0 else (ri >= ci)
        b_col = jnp.sum(jnp.where(mask, f_row, 0.0), axis=1, keepdims=True)
        b_row = jnp.sum(jnp.where(mask_t, f_col, 0.0), axis=0, keepdims=True)
        d_intra = jnp.where(mask, b_col - b_row + i_row, -jnp.inf)
        d_inter = b_col + m_prev
        m_row = jnp.maximum(d_inter, jnp.max(d_intra, axis=1, keepdims=True))
        w_intra = jnp.exp(d_intra - m_row)
        w_inter = jnp.exp(d_inter - m_row)
        last = L - 1 if d == 0 else 0
        m_new = m_row[last:last + 1, :]
        b_last = b_col[last:last + 1, :]
        w_state = jnp.exp(b_last - b_col + i_col - m_new)
        decay = jnp.exp(b_last + m_prev - m_new)
        kw = w_state * kf
        return dict(rows=rows, cols=cols, qf=qf, qb=qb, kb=kb, vb=vb, kw=kw, w_intra=w_intra, w_inter=w_inter,
                    m_row=m_row, m_new=m_new, decay=decay)

    def first_matmuls(g, state):
        c_prev = state[0]
        g["qk"] = lax.dot_general(g["qb"], g["kb"], _NT, preferred_element_type=F32)
        g["qc"] = jnp.dot(g["qb"], c_prev.astype(BF16), preferred_element_type=F32)
        g["kv"] = lax.dot_general(g["kw"].astype(BF16), g["vb"], (((0,), (0,)), ((), ())),
                                  preferred_element_type=F32)

    def second_matmul(g):
        g["s"] = g["qk"] * g["w_intra"]
        g["sv"] = jnp.dot(g["s"].astype(BF16), g["vb"], preferred_element_type=F32)

    def finish(d, g, state):
        c_prev, n_prev, _ = state
        num = g["sv"] + g["w_inter"] * g["qc"]
        den = (jnp.sum(g["s"], axis=1, keepdims=True)
               + g["w_inter"] * jnp.sum(g["qf"] * n_prev, axis=1, keepdims=True))
        hh = num / jnp.maximum(jnp.abs(den), jnp.exp(-g["m_row"]))
        if d == 0:
            hf_ref[g["rows"], g["cols"]] = hh
        else:
            hb_ref[g["rows"], g["cols"]] = hh
        c_new = g["decay"] * c_prev + g["kv"]
        n_new = g["decay"] * n_prev + jnp.sum(g["kw"], axis=0, keepdims=True)
        return c_new, n_new, g["m_new"]

    chains = [(d, a) for a in range(hp) for d in (0, 1)]

    def body(j, states):
        gs = [gates(d, a, j if d == 0 else nc - 1 - j, st) for (d, a), st in zip(chains, states)]
        for g, st in zip(gs, states):
            first_matmuls(g, st)
        for g in gs:
            second_matmul(g)
        return tuple(finish(d, g, st) for (d, _), g, st in zip(chains, gs, states))

    init = tuple((c0_ref[d, a], n0_ref[d, a], m0_ref[d, a]) for d, a in chains)
    final = lax.fori_loop(0, nc, body, init)
    for (d, a), (c_fin, n_fin, m_fin) in zip(chains, final):
        co_ref[d, a] = c_fin
        no_ref[d, a] = n_fin
        mo_ref[d, a] = m_fin
    for a in range(hp):
        cols = slice(a * LANE, (a + 1) * LANE)
        hs = hf_ref[:, cols] + hb_ref[:, cols]
        y = hs * lax.rsqrt(jnp.mean(hs * hs, axis=-1, keepdims=True) + NORM_EPS) * g_ref[a]
        y_ref[:, cols] = (y * jax.nn.sigmoid(og_ref[:, cols])).astype(y_ref.dtype)


def _mlstm(y_buf, proj, lay, gr, gc, bias, c0, n0, m0, gnorm, *, batch, seq, row0):
    nc = seq // MLSTM_CHUNK
    assert row0 % seq == 0
    rb0 = row0 // seq
    H = MLSTM_HEADS
    per_head = seq * LANE * 4 * (4 * 2 + 2 + 2)
    hp = max(h for h in (1, 2, 4) if h == 1 or h * per_head <= MLSTM_VMEM_BUDGET)
    wide = hp * LANE

    def pspec(name):
        cb = lay[name][0] // wide
        return pl.BlockSpec((seq, wide), lambda b, h: (rb0 + b, cb + h))

    st = lambda a, b_: pl.BlockSpec((None, 2, hp, a, b_), lambda b, h: (b, 0, h, 0, 0))
    kern = functools.partial(_mlstm_kernel, nc=nc, hp=hp)

    def body(*refs):
        kern(*refs[:11], *refs[12:])

    return pl.pallas_call(
        body, grid=(batch, H // hp),
        in_specs=[pl.BlockSpec(memory_space=pltpu.SMEM),
                  pspec("mq"), pspec("mk"), pspec("mv"), pspec("mo"),
                  pl.BlockSpec((hp, nc, 4, MLSTM_CHUNK), lambda b, h: (h, rb0 + b, 0, 0)),
                  pl.BlockSpec((hp, nc, MLSTM_CHUNK, 4), lambda b, h: (h, rb0 + b, 0, 0)),
                  st(MLSTM_DK, MLSTM_DV), st(1, MLSTM_DK), st(1, 1),
                  pl.BlockSpec((hp, 1, MLSTM_DV), lambda b, h: (h, 0, 0)),
                  pl.BlockSpec(memory_space=pl.ANY)],
        out_specs=[pl.BlockSpec((seq, wide), lambda b, h: (rb0 + b, h)),
                   st(MLSTM_DK, MLSTM_DV), st(1, MLSTM_DK), st(1, 1)],
        out_shape=[jax.ShapeDtypeStruct(y_buf.shape, y_buf.dtype),
                   jax.ShapeDtypeStruct((batch, 2, H, MLSTM_DK, MLSTM_DV), F32),
                   jax.ShapeDtypeStruct((batch, 2, H, 1, MLSTM_DK), F32),
                   jax.ShapeDtypeStruct((batch, 2, H, 1, 1), F32)],
        scratch_shapes=[pltpu.VMEM((seq, wide), F32), pltpu.VMEM((seq, wide), F32)],
        input_output_aliases={11: 0},
        compiler_params=_params(("parallel", "arbitrary")))(
            bias, proj, proj, proj, proj, gr, gc, c0, n0, m0, gnorm, y_buf)


def _merge_kernel(yb_ref, ya_ref, yc_ref, m0_ref, m1_ref, m2_ref, wb_ref, wa_ref, wc_ref, o_ref):
    z = jax.nn.sigmoid(m0_ref[...]) * jnp.dot(ya_ref[...], wa_ref[...], preferred_element_type=F32)
    z = z + jax.nn.sigmoid(m1_ref[...]) * jnp.dot(yb_ref[...], wb_ref[...], preferred_element_type=F32)
    z = z + jax.nn.sigmoid(m2_ref[...]) * jnp.dot(yc_ref[...], wc_ref[...], preferred_element_type=F32)
    o_ref[...] = z.astype(o_ref.dtype)


def _merge(ya, yb, yc, proj, lay, wbr, d_model):
    n = ya.shape[0]
    tm, tn = _tile(n, 512), _tile(d_model, 512)
    moff = lay["merge"][0] // tn
    nj = d_model // tn
    gate = lambda br: pl.BlockSpec((tm, tn), lambda i, j: (i, moff + br * nj + j))
    return pl.pallas_call(
        _merge_kernel, grid=(n // tm, nj),
        in_specs=[pl.BlockSpec((tm, GQA_WIDTH), lambda i, j: (i, 0)), pl.BlockSpec((tm, MLA_WIDTH), lambda i, j: (i, 0)),
                  pl.BlockSpec((tm, MLSTM_WIDTH), lambda i, j: (i, 0)), gate(0), gate(1), gate(2),
                  pl.BlockSpec((GQA_WIDTH, tn), lambda i, j: (0, j)),
                  pl.BlockSpec((MLA_WIDTH, tn), lambda i, j: (GQA_WIDTH // MLA_WIDTH, j)),
                  pl.BlockSpec((MLSTM_WIDTH, tn), lambda i, j: ((GQA_WIDTH + MLA_WIDTH) // MLSTM_WIDTH, j))],
        out_specs=pl.BlockSpec((tm, tn), lambda i, j: (i, j)),
        out_shape=jax.ShapeDtypeStruct((n, d_model), BF16),
        compiler_params=_params(("parallel", "parallel")))(yb, ya, yc, proj, proj, proj, wbr, wbr, wbr)


def _mm_res_kernel(z_ref, w_ref, x_ref, g_ref, o_ref):
    o_ref[...] = x_ref[...] + g_ref[...] * jnp.dot(z_ref[...], w_ref[...], preferred_element_type=F32)


def _matmul_residual(z, w, x, mod3, g_blk, n_prompt, dec_seq):
    n, k = z.shape
    d = w.shape[1]
    tm = _tile(np.gcd(n_prompt, dec_seq), 1024)
    tn = _tile(d, 512)
    nj = d // tn
    mrow = functools.partial(_mod_row, tm=tm, n_prompt=n_prompt, dec_seq=dec_seq)
    return pl.pallas_call(
        _mm_res_kernel, grid=(n // tm, nj),
        in_specs=[pl.BlockSpec((tm, k), lambda i, j: (i, 0)), pl.BlockSpec((k, tn), lambda i, j: (0, j)),
                  pl.BlockSpec((tm, tn), lambda i, j: (i, j)),
                  pl.BlockSpec((None, 1, tn), lambda i, j: (mrow(i), 0, g_blk * nj + j))],
        out_specs=pl.BlockSpec((tm, tn), lambda i, j: (i, j)),
        out_shape=jax.ShapeDtypeStruct((n, d), F32),
        compiler_params=_params(("parallel", "parallel")))(z, w, x, mod3)


def _router_kernel(x_ref, g_ref, sc_ref, sh_ref, rw_ref, rb_ref, h_ref, ti_ref, tg_ref, rk_ref, cnt_ref, carry_ref):
    @pl.when(pl.program_id(0) == 0)
    def _():
        carry_ref[...] = jnp.zeros_like(carry_ref)

    x = x_ref[...]
    r = lax.rsqrt(jnp.mean(x * x, axis=-1, keepdims=True) + NORM_EPS)
    h = (x * r * g_ref[...]) * (1.0 + sc_ref[...]) + sh_ref[...]
    hh = h.astype(BF16)
    half = h.shape[1] // 2
    bits = lax.bitcast_convert_type(hh.astype(F32), jnp.uint32)
    h_ref[...] = (bits[:, :half] | (bits[:, half:] >> 16)).reshape(h.shape[0], 1, half)
    hl = (h - hh.astype(F32)).astype(BF16)
    rw = rw_ref[...]
    rh = rw.astype(BF16)
    rl = (rw - rh.astype(F32)).astype(BF16)
    lg = (lax.dot_general(rh, hh, _NT, preferred_element_type=F32)
          + lax.dot_general(rh, hl, _NT, preferred_element_type=F32)
          + lax.dot_general(rl, hh, _NT, preferred_element_type=F32)) + rb_ref[...]
    n_exp, tm = lg.shape
    eidx = lax.broadcasted_iota(I32, lg.shape, 0).astype(F32)
    vals, idxs = [], []
    cur = lg
    for _k in range(TOP_K):
        m = jnp.max(cur, axis=0, keepdims=True)
        ix = jnp.min(jnp.where(cur == m, eidx, float(n_exp)), axis=0, keepdims=True)
        vals.append(m)
        idxs.append(ix)
        cur = jnp.where(eidx == ix, -jnp.inf, cur)
    ex = [jnp.exp(v - vals[0]) for v in vals]
    den = ex[0] + ex[1] + ex[2] + ex[3]
    onehot = jnp.zeros(lg.shape, F32)
    for ix in idxs:
        onehot = onehot + (eidx == ix).astype(F32)
    ti = lax.broadcasted_iota(I32, (tm, tm), 0)
    tj = lax.broadcasted_iota(I32, (tm, tm), 1)
    tri = (ti <= tj).astype(BF16)
    cum = jnp.dot(onehot.astype(BF16), tri, preferred_element_type=F32)
    excl = cum - onehot + carry_ref[...]
    for k in range(TOP_K):
        ti_ref[k:k + 1, :] = idxs[k].astype(I32)
        tg_ref[k:k + 1, :] = ex[k] / den
        rk_ref[k:k + 1, :] = jnp.sum(jnp.where(eidx == idxs[k], excl, 0.0), axis=0, keepdims=True).astype(I32)
    carry_ref[...] = carry_ref[...] + jnp.sum(onehot, axis=1, keepdims=True)
    cnt_ref[...] = carry_ref[...]


def _router(x, g, mod3, sc_blk, sh_blk, rw_t, rb, n_prompt, dec_seq):
    n, d = x.shape
    n_exp = rw_t.shape[0]
    tm = _tile(np.gcd(n_prompt, dec_seq), 256)
    mrow = functools.partial(_mod_row, tm=tm, n_prompt=n_prompt, dec_seq=dec_seq)
    kspec = pl.BlockSpec((TOP_K, tm), lambda i: (0, i))
    return pl.pallas_call(
        _router_kernel, grid=(n // tm,),
        in_specs=[pl.BlockSpec((tm, d), lambda i: (i, 0)), pl.BlockSpec((1, d), lambda i: (0, 0)),
                  pl.BlockSpec((None, 1, d), lambda i: (mrow(i), 0, sc_blk)),
                  pl.BlockSpec((None, 1, d), lambda i: (mrow(i), 0, sh_blk)),
                  pl.BlockSpec((n_exp, d), lambda i: (0, 0)), pl.BlockSpec((n_exp, 1), lambda i: (0, 0))],
        out_specs=[pl.BlockSpec((tm, 1, d // 2), lambda i: (i, 0, 0)), kspec, kspec, kspec,
                   pl.BlockSpec((n_exp, 1), lambda i: (0, 0))],
        out_shape=[jax.ShapeDtypeStruct((n, 1, d // 2), jnp.uint32), jax.ShapeDtypeStruct((TOP_K, n), I32),
                   jax.ShapeDtypeStruct((TOP_K, n), F32), jax.ShapeDtypeStruct((TOP_K, n), I32),
                   jax.ShapeDtypeStruct((n_exp, 1), F32)],
        scratch_shapes=[pltpu.VMEM((n_exp, 1), F32)],
        compiler_params=_params(("arbitrary",)))(x, g.reshape(1, d), mod3, mod3, rw_t, rb.reshape(n_exp, 1))


def _start_row_gather(idx_ref, src_ref, buf_ref, sem, rows):
    def issue(r8, carry):
        for u in range(DMA_UNROLL):
            r = r8 * DMA_UNROLL + u
            pltpu.make_async_copy(src_ref.at[idx_ref[0, r]], buf_ref.at[r], sem).start()
        return carry

    lax.fori_loop(0, rows // DMA_UNROLL, issue, 0)


def _wait_row_gather(src_ref, buf_ref, sem, rows):
    pltpu.make_async_copy(src_ref.at[pl.ds(0, rows)], buf_ref, sem).wait()


def _ffn_up_kernel(be_ref, na_ref, idx0_ref, idx1_ref, hp_ref, wg_ref, wu_ref, bg_ref, bu_ref, a_ref,
                   xg0_ref, xg1_ref, x2d_ref, xbf_ref, sem):
    j, f = pl.program_id(0), pl.program_id(1)
    na = na_ref[0]
    bm, half = x2d_ref.shape

    @pl.when((f == 0) & (j < na))
    def _():
        @pl.when(j == 0)
        def _():
            _start_row_gather(idx0_ref, hp_ref, xg0_ref, sem.at[0], bm)

        for par in (0, 1):
            @pl.when(j % 2 == par)
            def _(par=par):
                cur, nxt = (xg0_ref, xg1_ref) if par == 0 else (xg1_ref, xg0_ref)
                _wait_row_gather(hp_ref, cur, sem.at[par], bm)

                @pl.when(j + 1 < na)
                def _():
                    _start_row_gather(idx1_ref, hp_ref, nxt, sem.at[1 - par], bm)

                x2d_ref[...] = cur[...].reshape(bm, half)
                bits = x2d_ref[...]
                xbf_ref[:, :half] = lax.bitcast_convert_type(bits & jnp.uint32(0xFFFF0000), F32).astype(BF16)
                xbf_ref[:, half:] = lax.bitcast_convert_type(bits << 16, F32).astype(BF16)

    @pl.when(j < na)
    def _():
        x = xbf_ref[...]
        g = jnp.dot(x, wg_ref[...], preferred_element_type=F32) + bg_ref[...]
        u = jnp.dot(x, wu_ref[...], preferred_element_type=F32) + bu_ref[...]
        g = jnp.minimum(g, SWIGLU_LIMIT)
        u = jnp.clip(u, -SWIGLU_LIMIT, SWIGLU_LIMIT)
        a_ref[...] = ((u + 1.0) * g * jax.nn.sigmoid(SWIGLU_ALPHA * g)).astype(BF16)

    @pl.when(j >= na)
    def _():
        a_ref[...] = jnp.zeros_like(a_ref)


def _ffn_down_kernel(be_ref, na_ref, a_ref, wd_ref, bd_ref, o_ref):
    active = pl.program_id(0) < na_ref[0]

    @pl.when(active)
    def _():
        y = jnp.dot(a_ref[...], wd_ref[...], preferred_element_type=F32) + bd_ref[...]
        o_ref[...] = y.reshape(o_ref.shape)

    @pl.when(jnp.logical_not(active))
    def _():
        o_ref[...] = jnp.zeros_like(o_ref)


def _expert_ffn(hp, slot_tok, blk_expert, n_active, wg, wu, wd, bg, bu, bd, layer):
    n_slots = slot_tok.shape[0]
    d = 2 * hp.shape[2]
    n_exp, _, d_ff = wg.shape[1:]
    bm = MOE_ROWS
    nb = n_slots // bm
    tf = _tile(d_ff, 512)
    nf = d_ff // tf
    idx = slot_tok.reshape(nb, 1, bm)

    def jj(j, na):
        return jnp.minimum(j, na[0] - 1)

    def ff(j, f, na, last):
        return jnp.where(j < na[0], f, last)

    up_spec = pltpu.PrefetchScalarGridSpec(
        num_scalar_prefetch=2, grid=(nb, nf),
        in_specs=[pl.BlockSpec((None, 1, bm), lambda j, f, be, na: (jj(j, na), 0, 0), memory_space=pltpu.SMEM),
                  pl.BlockSpec((None, 1, bm), lambda j, f, be, na: (jnp.minimum(jj(j, na) + 1, nb - 1), 0, 0),
                               memory_space=pltpu.SMEM),
                  pl.BlockSpec(memory_space=pl.ANY),
                  pl.BlockSpec((None, None, d, tf), lambda j, f, be, na: (layer, be[jj(j, na)], 0, ff(j, f, na, nf - 1))),
                  pl.BlockSpec((None, None, d, tf), lambda j, f, be, na: (layer, be[jj(j, na)], 0, ff(j, f, na, nf - 1))),
                  pl.BlockSpec((None, None, 1, tf), lambda j, f, be, na: (layer, be[jj(j, na)], 0, ff(j, f, na, nf - 1))),
                  pl.BlockSpec((None, None, 1, tf), lambda j, f, be, na: (layer, be[jj(j, na)], 0, ff(j, f, na, nf - 1)))],
        out_specs=pl.BlockSpec((bm, tf), lambda j, f, be, na: (j, f)),
        scratch_shapes=[pltpu.VMEM((bm, 1, d // 2), jnp.uint32), pltpu.VMEM((bm, 1, d // 2), jnp.uint32),
                        pltpu.VMEM((bm, d // 2), jnp.uint32), pltpu.VMEM((bm, d), BF16),
                        pltpu.SemaphoreType.DMA((2,))])
    act = pl.pallas_call(
        _ffn_up_kernel, grid_spec=up_spec,
        out_shape=jax.ShapeDtypeStruct((n_slots, d_ff), BF16),
        compiler_params=_params(("arbitrary", "arbitrary")))(blk_expert, n_active, idx, idx, hp, wg, wu, bg, bu)

    tn = _tile(d, 2048)
    nn = d // tn
    down_spec = pltpu.PrefetchScalarGridSpec(
        num_scalar_prefetch=2, grid=(nb, nn),
        in_specs=[pl.BlockSpec((bm, d_ff), lambda j, c, be, na: (jj(j, na), 0)),
                  pl.BlockSpec((None, None, d_ff, tn), lambda j, c, be, na: (layer, be[jj(j, na)], 0, ff(j, c, na, nn - 1))),
                  pl.BlockSpec((None, None, 1, tn), lambda j, c, be, na: (layer, be[jj(j, na)], 0, ff(j, c, na, nn - 1)))],
        out_specs=pl.BlockSpec((bm, 1, tn), lambda j, c, be, na: (j, 0, c)))
    return pl.pallas_call(
        _ffn_down_kernel, grid_spec=down_spec,
        out_shape=jax.ShapeDtypeStruct((n_slots, 1, d), F32),
        compiler_params=_params(("arbitrary", "arbitrary")))(blk_expert, n_active, act, wd, bd)


def _combine_kernel(idx0_ref, idx1_ref, yb_ref, x_ref, g_ref, w_ref, o_ref, g0_ref, g1_ref, y2d_ref, sem):
    i, n = pl.program_id(0), pl.num_programs(0)
    rows, d = y2d_ref.shape
    tm = rows // TOP_K

    @pl.when(i == 0)
    def _():
        _start_row_gather(idx0_ref, yb_ref, g0_ref, sem.at[0], rows)

    for par in (0, 1):
        @pl.when(i % 2 == par)
        def _(par=par):
            cur, nxt = (g0_ref, g1_ref) if par == 0 else (g1_ref, g0_ref)
            _wait_row_gather(yb_ref, cur, sem.at[par], rows)

            @pl.when(i + 1 < n)
            def _():
                _start_row_gather(idx1_ref, yb_ref, nxt, sem.at[1 - par], rows)

            y2d_ref[...] = cur[...].reshape(rows, d)
            w = w_ref[...]
            y = w[:, 0:1] * y2d_ref[0:tm, :]
            for k in range(1, TOP_K):
                y = y + w[:, k:k + 1] * y2d_ref[k * tm:(k + 1) * tm, :]
            o_ref[...] = x_ref[...] + g_ref[...] * y


def _combine(x, yb, dest, gate, mod3, g_blk, n_prompt, dec_seq):
    n, d = x.shape
    tm = _tile(np.gcd(n_prompt, dec_seq), 128)
    nt = n // tm
    rows = TOP_K * tm
    idx = dest.reshape(TOP_K, nt, tm).transpose(1, 0, 2).reshape(nt, 1, rows)
    mrow = functools.partial(_mod_row, tm=tm, n_prompt=n_prompt, dec_seq=dec_seq)
    return pl.pallas_call(
        _combine_kernel, grid=(nt,),
        in_specs=[pl.BlockSpec((None, 1, rows), lambda i: (i, 0, 0), memory_space=pltpu.SMEM),
                  pl.BlockSpec((None, 1, rows), lambda i: (jnp.minimum(i + 1, nt - 1), 0, 0), memory_space=pltpu.SMEM),
                  pl.BlockSpec(memory_space=pl.ANY),
                  pl.BlockSpec((tm, d), lambda i: (i, 0)),
                  pl.BlockSpec((None, 1, d), lambda i: (mrow(i), 0, g_blk)),
                  pl.BlockSpec((tm, TOP_K), lambda i: (i, 0))],
        out_specs=pl.BlockSpec((tm, d), lambda i: (i, 0)),
        out_shape=jax.ShapeDtypeStruct((n, d), F32),
        scratch_shapes=[pltpu.VMEM((rows, 1, d), F32), pltpu.VMEM((rows, 1, d), F32), pltpu.VMEM((rows, d), F32),
                        pltpu.SemaphoreType.DMA((2,))],
        compiler_params=_params(("arbitrary",)))(idx, idx, yb, x, mod3, gate.T)


def _moe(x, p, layer, mod3, n_prompt, dec_seq):
    n, d = x.shape
    n_exp = p["router_w"].shape[-1]
    bm = MOE_ROWS
    h, topi, gate, rank, cnt = _router(x, p["norm_ffn"][layer], mod3, 4, 3, p["router_w"][layer].T,
                                       p["router_b"][layer], n_prompt, dec_seq)
    counts = cnt[:, 0].astype(I32)
    padded = (counts + bm - 1) // bm * bm
    pad_end = jnp.cumsum(padded)
    pad_start = pad_end - padded
    first = jnp.sum(jnp.where(topi[None] == jnp.arange(n_exp, dtype=I32)[:, None, None], pad_start[:, None, None], 0),
                    axis=0)
    dest = first + rank
    n_slots = n * TOP_K + n_exp * bm
    nb = n_slots // bm
    slot_tok = jnp.zeros((n_slots,), I32).at[dest.reshape(-1)].set(jnp.tile(jnp.arange(n, dtype=I32), TOP_K))
    blk_start = jnp.arange(nb, dtype=I32) * bm
    blk_expert = jnp.minimum(jnp.sum((pad_end[None, :] <= blk_start[:, None]).astype(I32), axis=1), n_exp - 1)
    n_active = (pad_end[-1:] // bm).astype(I32)
    yb = _expert_ffn(h, slot_tok, blk_expert, n_active, p["moe_w_gate"], p["moe_w_up"], p["moe_w_down"],
                     p["moe_b_gate"], p["moe_b_up"], p["moe_b_down"], layer)
    return _combine(x, yb, dest, gate, mod3, 5, n_prompt, dec_seq)


def kernel(x_prompt, x_sample, cache_mla_ckv, cache_mla_krope, cache_gqa_k, cache_gqa_v, state_mlstm_C, state_mlstm_n, state_mlstm_m, c, c_ctx, w_mod, b_mod, norm_attn, norm_ffn, w_in, mla_q_norm, mla_w_qb, mla_kv_norm, mla_w_kvb, gqa_q_norm, gqa_k_norm, mlstm_gate_bias, mlstm_out_norm, w_branch, w_out, router_w, router_b, moe_w_gate, moe_b_gate, moe_w_up, moe_b_up, moe_w_down, moe_b_down, final_norm):
    bp, sp, d = x_prompt.shape
    bs, ss, _ = x_sample.shape
    depth = w_mod.shape[0]
    past = cache_mla_ckv.shape[2]
    n_p, n_s = bp * sp, bs * ss
    n = n_p + n_s
    tk_s = ss + past
    H = MLSTM_HEADS
    lay, _ = _proj_layout(d)
    n_exp = router_w.shape[-1]

    x = jnp.concatenate([x_prompt.reshape(n_p, d), x_sample.reshape(n_s, d)], axis=0)
    cond = jnp.zeros((16, d), F32).at[0].set(c_ctx).at[1:1 + bs].set(c)
    tabs = _rope_tables(n_p, bs, ss, MLA_ROPE) + _rope_tables(n_p, bs, ss, GQA_HEAD_DIM)
    moe = dict(router_w=router_w, router_b=router_b, norm_ffn=norm_ffn,
               moe_w_gate=moe_w_gate.astype(BF16), moe_w_up=moe_w_up.astype(BF16), moe_w_down=moe_w_down.astype(BF16),
               moe_b_gate=moe_b_gate.reshape(depth, n_exp, 1, -1), moe_b_up=moe_b_up.reshape(depth, n_exp, 1, -1),
               moe_b_down=moe_b_down.reshape(depth, n_exp, 1, d))
    zc = jnp.zeros((bp, 2, H, MLSTM_DK, MLSTM_DV), F32)
    zn = jnp.zeros((bp, 2, H, 1, MLSTM_DK), F32)
    zm = jnp.zeros((bp, 2, H, 1, 1), F32)
    ctx = []
    for l in range(depth):
        mod3 = _modulation(cond, w_mod[l], b_mod[l]).reshape(16, 1, 6 * d)
        h = _norm_mod(x, norm_attn[l], mod3, 1, 0, n_p, ss)
        proj = _matmul(h, _build_w_in(w_in[l], d), F32, 1024, 512)
        qn = jnp.pad(mla_q_norm[l], (0, Q_LORA_PAD - MLA_Q_LORA)).reshape(1, Q_LORA_PAD)
        qm, ckv32, ckv16, kr16, qg, kg32, kg16, v16 = _prep(
            proj, lay, tabs, qn, mla_kv_norm[l].reshape(1, -1), gqa_q_norm[l].reshape(1, -1),
            gqa_k_norm[l].reshape(1, -1), _build_w_qb(mla_w_qb[l]))
        ckv_all = jnp.concatenate([
            jnp.concatenate([ckv16[n_p:].reshape(bs, ss, -1), cache_mla_ckv[:, l].astype(BF16)], axis=1).reshape(bs * tk_s, -1),
            ckv16[:n_p]], axis=0)
        kr_s = jnp.concatenate([kr16[n_p:].reshape(bs, ss, LANE),
                                jnp.pad(cache_mla_krope[:, l], ((0, 0), (0, 0), (0, LANE - MLA_ROPE))).astype(BF16)],
                               axis=1).reshape(bs * tk_s, LANE)
        km, vm = _kv_expand(ckv_all, jnp.concatenate([kr_s, kr16[:n_p]], axis=0), _build_w_kvb(mla_w_kvb[l]))
        kg_s = jnp.concatenate([kg16[n_p:].reshape(bs, ss, -1), cache_gqa_k[:, l].reshape(bs, past, -1).astype(BF16)],
                               axis=1).reshape(bs * tk_s, -1)
        vg_s = jnp.concatenate([v16[n_p:].reshape(bs, ss, -1), cache_gqa_v[:, l].reshape(bs, past, -1).astype(BF16)],
                               axis=1).reshape(bs * tk_s, -1)
        ya = jnp.zeros((n, MLA_WIDTH), BF16)
        mla = dict(heads=MLA_HEADS, dq=MLA_QK, shared_kv=False)
        ya = _attention_call(ya, qm, km, vm, batch=bp, tq_len=sp, tk_len=sp, hp=MLA_HEADS, tq_cap=256, q_row0=0,
                             k_row0=bs * tk_s, **mla)
        ya = _attention_call(ya, qm, km, vm, batch=bs, tq_len=ss, tk_len=tk_s, hp=2, tq_cap=256, q_row0=n_p,
                             k_row0=0, **mla)
        gqa = dict(heads=GQA_HEADS, hp=GQA_HEADS // GQA_KV_HEADS, dq=GQA_HEAD_DIM, tq_cap=256, shared_kv=True)
        yb = jnp.zeros((n, GQA_WIDTH), BF16)
        yb = _attention_call(yb, qg, kg16, v16, batch=bp, tq_len=sp, tk_len=sp, q_row0=0, k_row0=0, **gqa)
        yb = _attention_call(yb, qg, kg_s, vg_s, batch=bs, tq_len=ss, tk_len=tk_s, q_row0=n_p, k_row0=0, **gqa)
        mg0 = lay["mg"][0]
        mg = proj[:, mg0:mg0 + 4 * H].reshape(n // MLSTM_CHUNK, MLSTM_CHUNK, 4, H)
        gr = mg.transpose(3, 0, 2, 1)
        gc = mg.transpose(3, 0, 1, 2)
        bias = mlstm_gate_bias[l].reshape(-1)
        gnorm = mlstm_out_norm[l].reshape(H, 1, MLSTM_DV)
        yc = jnp.zeros((n, MLSTM_WIDTH), BF16)
        yc, c_new, n_new, m_new = _mlstm(yc, proj, lay, gr, gc, bias, zc, zn, zm, gnorm, batch=bp, seq=sp, row0=0)
        yc, _, _, _ = _mlstm(yc, proj, lay, gr, gc, bias, state_mlstm_C[:, l],
                             state_mlstm_n[:, l].reshape(bs, 2, H, 1, MLSTM_DK),
                             state_mlstm_m[:, l].reshape(bs, 2, H, 1, 1), gnorm, batch=bs, seq=ss, row0=n_p)
        wb = w_branch[l].astype(BF16)
        wbr = jnp.concatenate([wb[MLA_WIDTH:MLA_WIDTH + GQA_WIDTH], wb[:MLA_WIDTH], wb[MLA_WIDTH + GQA_WIDTH:]], axis=0)
        z = _merge(ya, yb, yc, proj, lay, wbr, d)
        x = _matmul_residual(z, w_out[l].astype(BF16), x, mod3, 2, n_p, ss)
        x = _moe(x, moe, l, mod3, n_p, ss)
        kr0, gv0 = lay["kr"][0], lay["gv"][0]
        ctx.append((ckv32[:n_p].reshape(bp, sp, -1), proj[:n_p, kr0:kr0 + MLA_ROPE].reshape(bp, sp, MLA_ROPE),
                    kg32[:n_p].reshape(bp, sp, GQA_KV_HEADS, GQA_HEAD_DIM),
                    proj[:n_p, gv0:gv0 + GQA_KV_WIDTH].reshape(bp, sp, GQA_KV_HEADS, GQA_HEAD_DIM),
                    c_new, n_new.reshape(bp, 2, H, MLSTM_DK), m_new.reshape(bp, 2, H)))
    y_p = _rmsnorm(x, final_norm, 0, n_p)
    y_s = _rmsnorm(x, final_norm, n_p, n_s)
    outs = [jnp.stack([t[i] for t in ctx], axis=1) for i in range(7)]
    return (y_p.reshape(bp, sp, d), y_s.reshape(bs, ss, d), *outs)
```

```python
import functools

import jax
import jax.numpy as jnp
import numpy as np
from jax import lax
from jax.experimental import pallas as pl
from jax.experimental.pallas import tpu as pltpu

F32 = jnp.float32
BF16 = jnp.bfloat16
I32 = jnp.int32

GRID_W = 64
ROPE_BASE = 10000.0
NORM_EPS = 1e-6

MLA_HEADS = 8
MLA_Q_LORA = 896
MLA_KV_LORA = 512
MLA_NOPE = 128
MLA_ROPE = 64
MLA_V = 128
GQA_HEADS = 16
GQA_KV_HEADS = 4
GQA_HEAD_DIM = 128
MLSTM_HEADS = 8
MLSTM_DK = 128
MLSTM_DV = 128
MLSTM_CHUNK = 64
N_BRANCHES = 3
TOP_K = 4
SWIGLU_LIMIT = 7.0
SWIGLU_ALPHA = 1.702

LANE = 128
VMEM_LIMIT = 56 * 1024 * 1024
MOE_ROWS = 512
DMA_UNROLL = 8
MLSTM_VMEM_BUDGET = 32 * 1024 * 1024

MLA_WIDTH = MLA_HEADS * MLA_V
GQA_WIDTH = GQA_HEADS * GQA_HEAD_DIM
GQA_KV_WIDTH = GQA_KV_HEADS * GQA_HEAD_DIM
MLSTM_WIDTH = MLSTM_HEADS * MLSTM_DV
Q_LORA_PAD = 1024
MLA_QK = 2 * LANE


def _tile(n, cap):
    for t in (2048, 1024, 512, 256, 128, 64, 32, 16, 8):
        if t <= cap and n % t == 0:
            return t
    raise ValueError(f"no tile for {n}")


def _params(sem, vmem=VMEM_LIMIT):
    return pltpu.CompilerParams(dimension_semantics=sem, vmem_limit_bytes=vmem)


def _proj_layout(d_model):
    segs = [("merge", N_BRANCHES * d_model, N_BRANCHES * d_model), ("gq", GQA_WIDTH, GQA_WIDTH),
            ("mq", MLSTM_WIDTH, MLSTM_WIDTH), ("mk", MLSTM_WIDTH, MLSTM_WIDTH), ("mv", MLSTM_WIDTH, MLSTM_WIDTH),
            ("mo", MLSTM_WIDTH, MLSTM_WIDTH), ("qa", MLA_Q_LORA, Q_LORA_PAD), ("kva", MLA_KV_LORA, MLA_KV_LORA),
            ("gk", GQA_KV_WIDTH, GQA_KV_WIDTH), ("gv", GQA_KV_WIDTH, GQA_KV_WIDTH),
            ("kr", MLA_ROPE, LANE), ("mg", 4 * MLSTM_HEADS, LANE)]
    off, out = 0, {}
    for name, w, wp in segs:
        assert off % wp == 0, (name, off, wp)
        out[name] = (off, w, wp)
        off += wp
    total = -(-off // 512) * 512
    return out, total


def _build_w_in(w_in_l, d_model):
    lay, total = _proj_layout(d_model)
    ref_sizes = dict(qa=MLA_Q_LORA, kva=MLA_KV_LORA, kr=MLA_ROPE, gq=GQA_WIDTH, gk=GQA_KV_WIDTH, gv=GQA_KV_WIDTH,
                     mq=MLSTM_WIDTH, mk=MLSTM_WIDTH, mv=MLSTM_WIDTH, mo=MLSTM_WIDTH, mg=4 * MLSTM_HEADS,
                     merge=N_BRANCHES * d_model)
    ref_order = ["qa", "kva", "kr", "gq", "gk", "gv", "mq", "mk", "mv", "mo", "mg", "merge"]
    src, o = {}, 0
    for name in ref_order:
        src[name] = (o, ref_sizes[name])
        o += ref_sizes[name]
    wb = w_in_l.astype(BF16)
    pieces, used = [], 0
    for name, (off, w, wp) in lay.items():
        s0, sw = src[name]
        pieces.append(wb[:, s0:s0 + sw])
        if wp > w:
            pieces.append(jnp.zeros((wb.shape[0], wp - w), BF16))
        used = off + wp
    if total > used:
        pieces.append(jnp.zeros((wb.shape[0], total - used), BF16))
    return jnp.concatenate(pieces, axis=1)


def _build_w_qb(w_qb_l):
    w = w_qb_l.astype(BF16).reshape(MLA_Q_LORA, MLA_HEADS, MLA_NOPE + MLA_ROPE)
    w = jnp.pad(w, ((0, Q_LORA_PAD - MLA_Q_LORA), (0, 0), (0, MLA_QK - MLA_NOPE - MLA_ROPE)))
    return w.reshape(Q_LORA_PAD, MLA_HEADS * MLA_QK)


def _build_w_kvb(w_kvb_l):
    w = w_kvb_l.astype(BF16).reshape(MLA_KV_LORA, MLA_HEADS, MLA_NOPE + MLA_V)
    return jnp.concatenate([w[:, :, :MLA_NOPE].reshape(MLA_KV_LORA, -1), w[:, :, MLA_NOPE:].reshape(MLA_KV_LORA, -1)],
                           axis=1)


def _rope_tables(n_prompt, dec_batch, dec_seq, rot_dim):
    quarter = rot_dim // 4
    grid_rows = dec_seq // GRID_W
    row = jnp.repeat(jnp.arange(grid_rows, dtype=F32), GRID_W)
    col = jnp.tile(jnp.arange(GRID_W, dtype=F32), grid_rows)
    inv_freq = ROPE_BASE ** (-jnp.arange(quarter, dtype=F32) / quarter)
    ang_r = row[:, None] * inv_freq
    ang_c = col[:, None] * inv_freq
    ang = jnp.concatenate([ang_r, ang_r, ang_c, ang_c], axis=-1)
    cos = jnp.pad(jnp.cos(ang), ((0, 0), (0, LANE - rot_dim)), constant_values=1.0)
    sin = jnp.pad(jnp.sin(ang), ((0, 0), (0, LANE - rot_dim)))
    cos = jnp.concatenate([jnp.ones((n_prompt, LANE), F32), jnp.tile(cos, (dec_batch, 1))], axis=0)
    sin = jnp.concatenate([jnp.zeros((n_prompt, LANE), F32), jnp.tile(sin, (dec_batch, 1))], axis=0)
    return cos, sin


def _mod_kernel(c_ref, w_ref, b_ref, o_ref):
    c = c_ref[...]
    a = (c * jax.nn.sigmoid(c)).astype(BF16)
    o_ref[...] = jnp.dot(a, w_ref[...].astype(BF16), preferred_element_type=F32) + b_ref[...]


def _modulation(cond, w_mod_l, b_mod_l):
    rows, d = cond.shape
    n = w_mod_l.shape[1]
    tn = _tile(n, 512)
    return pl.pallas_call(
        _mod_kernel, grid=(n // tn,),
        in_specs=[pl.BlockSpec((rows, d), lambda j: (0, 0)), pl.BlockSpec((d, tn), lambda j: (0, j)),
                  pl.BlockSpec((1, tn), lambda j: (0, j))],
        out_specs=pl.BlockSpec((rows, tn), lambda j: (0, j)),
        out_shape=jax.ShapeDtypeStruct((rows, n), F32),
        compiler_params=_params(("parallel",)))(cond, w_mod_l, b_mod_l.reshape(1, n))


def _mm_kernel(x_ref, w_ref, o_ref):
    o_ref[...] = jnp.dot(x_ref[...], w_ref[...], preferred_element_type=F32).astype(o_ref.dtype)


def _matmul(x, w, out_dtype, tm_cap, tn_cap):
    m, k = x.shape
    n = w.shape[1]
    tm, tn = _tile(m, tm_cap), _tile(n, tn_cap)
    return pl.pallas_call(
        _mm_kernel, grid=(m // tm, n // tn),
        in_specs=[pl.BlockSpec((tm, k), lambda i, j: (i, 0)), pl.BlockSpec((k, tn), lambda i, j: (0, j))],
        out_specs=pl.BlockSpec((tm, tn), lambda i, j: (i, j)),
        out_shape=jax.ShapeDtypeStruct((m, n), out_dtype),
        compiler_params=_params(("parallel", "parallel")))(x, w)


def _mod_row(i, tm, n_prompt, dec_seq):
    r0 = i * tm
    return jnp.where(r0 < n_prompt, 0, 1 + (r0 - n_prompt) // dec_seq)


def _norm_mod_kernel(x_ref, g_ref, sc_ref, sh_ref, o_ref):
    x = x_ref[...]
    r = lax.rsqrt(jnp.mean(x * x, axis=-1, keepdims=True) + NORM_EPS)
    y = x * r * g_ref[...]
    o_ref[...] = (y * (1.0 + sc_ref[...]) + sh_ref[...]).astype(o_ref.dtype)


def _norm_mod(x, g, mod3, sc_blk, sh_blk, n_prompt, dec_seq):
    n, d = x.shape
    tm = _tile(np.gcd(n_prompt, dec_seq), 256)
    mrow = functools.partial(_mod_row, tm=tm, n_prompt=n_prompt, dec_seq=dec_seq)
    return pl.pallas_call(
        _norm_mod_kernel, grid=(n // tm,),
        in_specs=[pl.BlockSpec((tm, d), lambda i: (i, 0)), pl.BlockSpec((1, d), lambda i: (0, 0)),
                  pl.BlockSpec((None, 1, d), lambda i: (mrow(i), 0, sc_blk)),
                  pl.BlockSpec((None, 1, d), lambda i: (mrow(i), 0, sh_blk))],
        out_specs=pl.BlockSpec((tm, d), lambda i: (i, 0)),
        out_shape=jax.ShapeDtypeStruct((n, d), BF16),
        compiler_params=_params(("parallel",)))(x, g.reshape(1, d), mod3, mod3)


def _rms_kernel(x_ref, g_ref, o_ref):
    x = x_ref[...]
    r = lax.rsqrt(jnp.mean(x * x, axis=-1, keepdims=True) + NORM_EPS)
    o_ref[...] = x * r * g_ref[...]


def _rmsnorm(x, g, row0, rows):
    d = x.shape[1]
    tm = _tile(np.gcd(row0, rows) if row0 else rows, 256)
    b0 = row0 // tm
    return pl.pallas_call(
        _rms_kernel, grid=(rows // tm,),
        in_specs=[pl.BlockSpec((tm, d), lambda i: (b0 + i, 0)), pl.BlockSpec((1, d), lambda i: (0, 0))],
        out_specs=pl.BlockSpec((tm, d), lambda i: (i, 0)),
        out_shape=jax.ShapeDtypeStruct((rows, d), F32),
        compiler_params=_params(("parallel",)))(x, g.reshape(1, d))


def _rot_half(x, half):
    lane = lax.broadcasted_iota(I32, x.shape, 1)
    first = (lane % (2 * half)) < half
    return jnp.where(first, -pltpu.roll(x, LANE - half, 1), pltpu.roll(x, half, 1))


def _prep_kernel(qa_ref, kva_ref, gq_ref, gk_ref, gv_ref, kr_ref, ca_ref, sa_ref, cb_ref, sb_ref,
                 qn_ref, kvn_ref, gqn_ref, gkn_ref, wqb_ref,
                 qm_ref, ckv32_ref, ckv16_ref, kr16_ref, qg_ref, kg32_ref, kg16_ref, v16_ref):
    ca, sa, cb, sb = ca_ref[...], sa_ref[...], cb_ref[...], sb_ref[...]
    qa = qa_ref[...]
    r = lax.rsqrt(jnp.sum(qa * qa, axis=-1, keepdims=True) * (1.0 / MLA_Q_LORA) + NORM_EPS)
    qn = (qa * r * qn_ref[...]).astype(BF16)
    q = jnp.dot(qn, wqb_ref[...], preferred_element_type=F32)
    scale_a = (MLA_NOPE + MLA_ROPE) ** -0.5
    for h in range(MLA_HEADS):
        c0 = h * MLA_QK
        qm_ref[:, c0:c0 + LANE] = (q[:, c0:c0 + LANE] * scale_a).astype(BF16)
        x = q[:, c0 + LANE:c0 + 2 * LANE]
        qm_ref[:, c0 + LANE:c0 + 2 * LANE] = ((x * ca + _rot_half(x, MLA_ROPE // 4) * sa) * scale_a).astype(BF16)
    kva = kva_ref[...]
    r = lax.rsqrt(jnp.mean(kva * kva, axis=-1, keepdims=True) + NORM_EPS)
    ckv = kva * r * kvn_ref[...]
    ckv32_ref[...] = ckv
    ckv16_ref[...] = ckv.astype(BF16)
    kr = kr_ref[...]
    kr16_ref[...] = (kr * ca + _rot_half(kr, MLA_ROPE // 4) * sa).astype(BF16)
    scale_b = GQA_HEAD_DIM ** -0.5
    gqn, gkn = gqn_ref[...], gkn_ref[...]
    for h in range(GQA_HEADS):
        sl = slice(h * LANE, (h + 1) * LANE)
        x = gq_ref[:, sl]
        x = x * lax.rsqrt(jnp.mean(x * x, axis=-1, keepdims=True) + NORM_EPS) * gqn
        qg_ref[:, sl] = ((x * cb + _rot_half(x, GQA_HEAD_DIM // 4) * sb) * scale_b).astype(BF16)
    for h in range(GQA_KV_HEADS):
        sl = slice(h * LANE, (h + 1) * LANE)
        x = gk_ref[:, sl]
        x = x * lax.rsqrt(jnp.mean(x * x, axis=-1, keepdims=True) + NORM_EPS) * gkn
        kg32_ref[:, sl] = x
        kg16_ref[:, sl] = (x * cb + _rot_half(x, GQA_HEAD_DIM // 4) * sb).astype(BF16)
    v16_ref[...] = gv_ref[...].astype(BF16)


def _prep(proj, lay, tabs, qn, kvn, gqn, gkn, wqb):
    n = proj.shape[0]
    tm = _tile(n, 256)
    ca, sa, cb, sb = tabs

    def pspec(name):
        off, _, wp = lay[name]
        return pl.BlockSpec((tm, wp), lambda i: (i, off // wp))

    def row(w):
        return pl.BlockSpec((tm, w), lambda i: (i, 0))

    def full(a):
        return pl.BlockSpec(a.shape, lambda i: (0,) * a.ndim)

    outs = [(2 * MLA_HEADS * LANE, BF16), (MLA_KV_LORA, F32), (MLA_KV_LORA, BF16), (LANE, BF16),
            (GQA_WIDTH, BF16), (GQA_KV_WIDTH, F32), (GQA_KV_WIDTH, BF16), (GQA_KV_WIDTH, BF16)]
    return pl.pallas_call(
        _prep_kernel, grid=(n // tm,),
        in_specs=[pspec("qa"), pspec("kva"), pspec("gq"), pspec("gk"), pspec("gv"), pspec("kr"),
                  row(LANE), row(LANE), row(LANE), row(LANE), full(qn), full(kvn), full(gqn), full(gkn), full(wqb)],
        out_specs=[row(w) for w, _ in outs],
        out_shape=[jax.ShapeDtypeStruct((n, w), dt) for w, dt in outs],
        compiler_params=_params(("parallel",)))(proj, proj, proj, proj, proj, proj, ca, sa, cb, sb,
                                                 qn, kvn, gqn, gkn, wqb)


def _kv_expand_kernel(ckv_ref, kr_ref, wk_ref, wv_ref, k_ref, v_ref):
    ckv = ckv_ref[...]
    k = jnp.dot(ckv, wk_ref[...], preferred_element_type=F32).astype(BF16)
    kr = kr_ref[...]
    for h in range(MLA_HEADS):
        k_ref[:, h * MLA_QK:h * MLA_QK + LANE] = k[:, h * LANE:(h + 1) * LANE]
        k_ref[:, h * MLA_QK + LANE:(h + 1) * MLA_QK] = kr
    v_ref[...] = jnp.dot(ckv, wv_ref[...], preferred_element_type=F32).astype(BF16)


def _kv_expand(ckv, kr, w_kvb):
    rows, lora = ckv.shape
    tm = _tile(rows, 512)
    kw, vw = MLA_HEADS * MLA_NOPE, MLA_HEADS * MLA_V
    assert kw == vw
    return pl.pallas_call(
        _kv_expand_kernel, grid=(rows // tm,),
        in_specs=[pl.BlockSpec((tm, lora), lambda i: (i, 0)), pl.BlockSpec((tm, LANE), lambda i: (i, 0)),
                  pl.BlockSpec((lora, kw), lambda i: (0, 0)), pl.BlockSpec((lora, vw), lambda i: (0, 1))],
        out_specs=[pl.BlockSpec((tm, MLA_HEADS * MLA_QK), lambda i: (i, 0)), pl.BlockSpec((tm, vw), lambda i: (i, 0))],
        out_shape=[jax.ShapeDtypeStruct((rows, MLA_HEADS * MLA_QK), BF16), jax.ShapeDtypeStruct((rows, vw), BF16)],
        compiler_params=_params(("parallel",)))(ckv, kr, w_kvb, w_kvb)


_NT = (((1,), (1,)), ((), ()))


def _attn_kernel(q_ref, k_ref, v_ref, _, o_ref, *, hp, dq, shared_kv):
    for a in range(hp):
        kv = 0 if shared_kv else a
        s = lax.dot_general(q_ref[:, a * dq:(a + 1) * dq], k_ref[:, kv * dq:(kv + 1) * dq], _NT,
                            preferred_element_type=F32)
        m = jnp.max(s, axis=-1, keepdims=True)
        p = jnp.exp(s - m)
        l = jnp.sum(p, axis=-1, keepdims=True)
        o = jnp.dot(p.astype(BF16), v_ref[:, kv * LANE:(kv + 1) * LANE], preferred_element_type=F32)
        o_ref[:, a * LANE:(a + 1) * LANE] = (o / l).astype(o_ref.dtype)


def _attention_call(out_buf, q, k, v, *, batch, tq_len, tk_len, heads, hp, dq, tq_cap, shared_kv, q_row0, k_row0):
    tq = _tile(tq_len, tq_cap)
    nq = tq_len // tq
    assert q_row0 % tq == 0 and k_row0 % tk_len == 0 and heads % hp == 0
    qb0, kb0 = q_row0 // tq, k_row0 // tk_len
    kvh = 1 if shared_kv else hp
    return pl.pallas_call(
        functools.partial(_attn_kernel, hp=hp, dq=dq, shared_kv=shared_kv),
        grid=(batch, heads // hp, nq),
        in_specs=[pl.BlockSpec((tq, hp * dq), lambda b, g, i: (qb0 + b * nq + i, g)),
                  pl.BlockSpec((tk_len, kvh * dq), lambda b, g, i: (kb0 + b, g)),
                  pl.BlockSpec((tk_len, kvh * LANE), lambda b, g, i: (kb0 + b, g)),
                  pl.BlockSpec(memory_space=pl.ANY)],
        out_specs=pl.BlockSpec((tq, hp * LANE), lambda b, g, i: (qb0 + b * nq + i, g)),
        out_shape=jax.ShapeDtypeStruct(out_buf.shape, out_buf.dtype),
        input_output_aliases={3: 0},
        compiler_params=_params(("parallel", "parallel", "arbitrary")))(q, k, v, out_buf)


def _log_sigmoid(x):
    return jnp.minimum(x, 0.0) - jnp.log(1.0 + jnp.exp(-jnp.abs(x)))


def _mlstm_kernel(bias_ref, q_ref, k_ref, v_ref, og_ref, gr_ref, gc_ref, c0_ref, n0_ref, m0_ref, g_ref,
                  y_ref, co_ref, no_ref, mo_ref, hf_ref, hb_ref, *, nc, hp):
    L = MLSTM_CHUNK
    ri = lax.broadcasted_iota(I32, (L, L), 0)
    ci = lax.broadcasted_iota(I32, (L, L), 1)

    def gates(d, a, c, state):
        _, _, m_prev = state
        head = pl.program_id(1) * hp + a
        cols = slice(a * LANE, (a + 1) * LANE)
        rows = pl.ds(pl.multiple_of(c * L, L), L)
        qf = q_ref[rows, cols] * (MLSTM_DK ** -0.5)
        kf = k_ref[rows, cols]
        qb, kb, vb = qf.astype(BF16), kf.astype(BF16), v_ref[rows, cols].astype(BF16)
        gr = gr_ref[a, c]
        gc = gc_ref[a, c]
        bi = bias_ref[2 * d * MLSTM_HEADS + head]
        bf = bias_ref[(2 * d + 1) * MLSTM_HEADS + head]
        i_row = gr[2 * d:2 * d + 1, :] + bi
        f_row = _log_sigmoid(gr[2 * d + 1:2 * d + 2, :] + bf)
        i_col = gc[:, 2 * d:2 * d + 1] + bi
        f_col = _log_sigmoid(gc[:, 2 * d + 1:2 * d + 2] + bf)
        mask = (ci <= ri) if d == 0 else (ci >= ri)
        mask_t = (ri <= ci) if d == 0 else (ri >= ci)
        b_col = jnp.sum(jnp.where(mask, f_row, 0.0), axis=1, keepdims=True)
        b_row = jnp.sum(jnp.where(mask_t, f_col, 0.0), axis=0, keepdims=True)
        d_intra = jnp.where(mask, b_col - b_row + i_row, -jnp.inf)
        d_inter = b_col + m_prev
        m_row = jnp.maximum(d_inter, jnp.max(d_intra, axis=1, keepdims=True))
        w_intra = jnp.exp(d_intra - m_row)
        w_inter = jnp.exp(d_inter - m_row)
        last = L - 1 if d == 0 else 0
        m_new = m_row[last:last + 1, :]
        b_last = b_col[last:last + 1, :]
        w_state = jnp.exp(b_last - b_col + i_col - m_new)
        decay = jnp.exp(b_last + m_prev - m_new)
        kw = w_state * kf
        return dict(rows=rows, cols=cols, qf=qf, qb=qb, kb=kb, vb=vb, kw=kw, w_intra=w_intra, w_inter=w_inter,
                    m_row=m_row, m_new=m_new, decay=decay)

    def first_matmuls(g, state):
        c_prev = state[0]
        g["qk"] = lax.dot_general(g["qb"], g["kb"], _NT, preferred_element_type=F32)
        g["qc"] = jnp.dot(g["qb"], c_prev.astype(BF16), preferred_element_type=F32)
        g["kv"] = lax.dot_general(g["kw"].astype(BF16), g["vb"], (((0,), (0,)), ((), ())),
                                  preferred_element_type=F32)

    def second_matmul(g):
        g["s"] = g["qk"] * g["w_intra"]
        g["sv"] = jnp.dot(g["s"].astype(BF16), g["vb"], preferred_element_type=F32)

    def finish(d, g, state):
        c_prev, n_prev, _ = state
        num = g["sv"] + g["w_inter"] * g["qc"]
        den = (jnp.sum(g["s"], axis=1, keepdims=True)
               + g["w_inter"] * jnp.sum(g["qf"] * n_prev, axis=1, keepdims=True))
        hh = num / jnp.maximum(jnp.abs(den), jnp.exp(-g["m_row"]))
        if d == 0:
            hf_ref[g["rows"], g["cols"]] = hh
        else:
            hb_ref[g["rows"], g["cols"]] = hh
        c_new = g["decay"] * c_prev + g["kv"]
        n_new = g["decay"] * n_prev + jnp.sum(g["kw"], axis=0, keepdims=True)
        return c_new, n_new, g["m_new"]

    chains = [(d, a) for a in range(hp) for d in (0, 1)]

    def body(j, states):
        gs = [gates(d, a, j if d == 0 else nc - 1 - j, st) for (d, a), st in zip(chains, states)]
        for g, st in zip(gs, states):
            first_matmuls(g, st)
        for g in gs:
            second_matmul(g)
        return tuple(finish(d, g, st) for (d, _), g, st in zip(chains, gs, states))

    init = tuple((c0_ref[d, a], n0_ref[d, a], m0_ref[d, a]) for d, a in chains)
    final = lax.fori_loop(0, nc, body, init)
    for (d, a), (c_fin, n_fin, m_fin) in zip(chains, final):
        co_ref[d, a] = c_fin
        no_ref[d, a] = n_fin
        mo_ref[d, a] = m_fin
    for a in range(hp):
        cols = slice(a * LANE, (a + 1) * LANE)
        hs = hf_ref[:, cols] + hb_ref[:, cols]
        y = hs * lax.rsqrt(jnp.mean(hs * hs, axis=-1, keepdims=True) + NORM_EPS) * g_ref[a]
        y_ref[:, cols] = (y * jax.nn.sigmoid(og_ref[:, cols])).astype(y_ref.dtype)


def _mlstm(y_buf, proj, lay, gr, gc, bias, c0, n0, m0, gnorm, *, batch, seq, row0):
    nc = seq // MLSTM_CHUNK
    assert row0 % seq == 0
    rb0 = row0 // seq
    H = MLSTM_HEADS
    per_head = seq * LANE * 4 * (4 * 2 + 2 + 2)
    hp = max(h for h in (1, 2, 4) if h == 1 or h * per_head <= MLSTM_VMEM_BUDGET)
    wide = hp * LANE

    def pspec(name):
        cb = lay[name][0] // wide
        return pl.BlockSpec((seq, wide), lambda b, h: (rb0 + b, cb + h))

    st = lambda a, b_: pl.BlockSpec((None, 2, hp, a, b_), lambda b, h: (b, 0, h, 0, 0))
    kern = functools.partial(_mlstm_kernel, nc=nc, hp=hp)

    def body(*refs):
        kern(*refs[:11], *refs[12:])

    return pl.pallas_call(
        body, grid=(batch, H // hp),
        in_specs=[pl.BlockSpec(memory_space=pltpu.SMEM),
                  pspec("mq"), pspec("mk"), pspec("mv"), pspec("mo"),
                  pl.BlockSpec((hp, nc, 4, MLSTM_CHUNK), lambda b, h: (h, rb0 + b, 0, 0)),
                  pl.BlockSpec((hp, nc, MLSTM_CHUNK, 4), lambda b, h: (h, rb0 + b, 0, 0)),
                  st(MLSTM_DK, MLSTM_DV), st(1, MLSTM_DK), st(1, 1),
                  pl.BlockSpec((hp, 1, MLSTM_DV), lambda b, h: (h, 0, 0)),
                  pl.BlockSpec(memory_space=pl.ANY)],
        out_specs=[pl.BlockSpec((seq, wide), lambda b, h: (rb0 + b, h)),
                   st(MLSTM_DK, MLSTM_DV), st(1, MLSTM_DK), st(1, 1)],
        out_shape=[jax.ShapeDtypeStruct(y_buf.shape, y_buf.dtype),
                   jax.ShapeDtypeStruct((batch, 2, H, MLSTM_DK, MLSTM_DV), F32),
                   jax.ShapeDtypeStruct((batch, 2, H, 1, MLSTM_DK), F32),
                   jax.ShapeDtypeStruct((batch, 2, H, 1, 1), F32)],
        scratch_shapes=[pltpu.VMEM((seq, wide), F32), pltpu.VMEM((seq, wide), F32)],
        input_output_aliases={11: 0},
        compiler_params=_params(("parallel", "arbitrary")))(
            bias, proj, proj, proj, proj, gr, gc, c0, n0, m0, gnorm, y_buf)


def _merge_kernel(yb_ref, ya_ref, yc_ref, m0_ref, m1_ref, m2_ref, wb_ref, wa_ref, wc_ref, o_ref):
    z = jax.nn.sigmoid(m0_ref[...]) * jnp.dot(ya_ref[...], wa_ref[...], preferred_element_type=F32)
    z = z + jax.nn.sigmoid(m1_ref[...]) * jnp.dot(yb_ref[...], wb_ref[...], preferred_element_type=F32)
    z = z + jax.nn.sigmoid(m2_ref[...]) * jnp.dot(yc_ref[...], wc_ref[...], preferred_element_type=F32)
    o_ref[...] = z.astype(o_ref.dtype)


def _merge(ya, yb, yc, proj, lay, wbr, d_model):
    n = ya.shape[0]
    tm, tn = _tile(n, 512), _tile(d_model, 512)
    moff = lay["merge"][0] // tn
    nj = d_model // tn
    gate = lambda br: pl.BlockSpec((tm, tn), lambda i, j: (i, moff + br * nj + j))
    return pl.pallas_call(
        _merge_kernel, grid=(n // tm, nj),
        in_specs=[pl.BlockSpec((tm, GQA_WIDTH), lambda i, j: (i, 0)), pl.BlockSpec((tm, MLA_WIDTH), lambda i, j: (i, 0)),
                  pl.BlockSpec((tm, MLSTM_WIDTH), lambda i, j: (i, 0)), gate(0), gate(1), gate(2),
                  pl.BlockSpec((GQA_WIDTH, tn), lambda i, j: (0, j)),
                  pl.BlockSpec((MLA_WIDTH, tn), lambda i, j: (GQA_WIDTH // MLA_WIDTH, j)),
                  pl.BlockSpec((MLSTM_WIDTH, tn), lambda i, j: ((GQA_WIDTH + MLA_WIDTH) // MLSTM_WIDTH, j))],
        out_specs=pl.BlockSpec((tm, tn), lambda i, j: (i, j)),
        out_shape=jax.ShapeDtypeStruct((n, d_model), BF16),
        compiler_params=_params(("parallel", "parallel")))(yb, ya, yc, proj, proj, proj, wbr, wbr, wbr)


def _mm_res_kernel(z_ref, w_ref, x_ref, g_ref, o_ref):
    o_ref[...] = x_ref[...] + g_ref[...] * jnp.dot(z_ref[...], w_ref[...], preferred_element_type=F32)


def _matmul_residual(z, w, x, mod3, g_blk, n_prompt, dec_seq):
    n, k = z.shape
    d = w.shape[1]
    tm = _tile(np.gcd(n_prompt, dec_seq), 1024)
    tn = _tile(d, 512)
    nj = d // tn
    mrow = functools.partial(_mod_row, tm=tm, n_prompt=n_prompt, dec_seq=dec_seq)
    return pl.pallas_call(
        _mm_res_kernel, grid=(n // tm, nj),
        in_specs=[pl.BlockSpec((tm, k), lambda i, j: (i, 0)), pl.BlockSpec((k, tn), lambda i, j: (0, j)),
                  pl.BlockSpec((tm, tn), lambda i, j: (i, j)),
                  pl.BlockSpec((None, 1, tn), lambda i, j: (mrow(i), 0, g_blk * nj + j))],
        out_specs=pl.BlockSpec((tm, tn), lambda i, j: (i, j)),
        out_shape=jax.ShapeDtypeStruct((n, d), F32),
        compiler_params=_params(("parallel", "parallel")))(z, w, x, mod3)


def _router_kernel(x_ref, g_ref, sc_ref, sh_ref, rw_ref, rb_ref, h_ref, ti_ref, tg_ref, rk_ref, cnt_ref, carry_ref):
    @pl.when(pl.program_id(0) == 0)
    def _():
        carry_ref[...] = jnp.zeros_like(carry_ref)

    x = x_ref[...]
    r = lax.rsqrt(jnp.mean(x * x, axis=-1, keepdims=True) + NORM_EPS)
    h = (x * r * g_ref[...]) * (1.0 + sc_ref[...]) + sh_ref[...]
    hh = h.astype(BF16)
    half = h.shape[1] // 2
    bits = lax.bitcast_convert_type(hh.astype(F32), jnp.uint32)
    h_ref[...] = (bits[:, :half] | (bits[:, half:] >> 16)).reshape(h.shape[0], 1, half)
    hl = (h - hh.astype(F32)).astype(BF16)
    rw = rw_ref[...]
    rh = rw.astype(BF16)
    rl = (rw - rh.astype(F32)).astype(BF16)
    lg = (lax.dot_general(rh, hh, _NT, preferred_element_type=F32)
          + lax.dot_general(rh, hl, _NT, preferred_element_type=F32)
          + lax.dot_general(rl, hh, _NT, preferred_element_type=F32)) + rb_ref[...]
    n_exp, tm = lg.shape
    eidx = lax.broadcasted_iota(I32, lg.shape, 0).astype(F32)
    vals, idxs = [], []
    cur = lg
    for _k in range(TOP_K):
        m = jnp.max(cur, axis=0, keepdims=True)
        ix = jnp.min(jnp.where(cur == m, eidx, float(n_exp)), axis=0, keepdims=True)
        vals.append(m)
        idxs.append(ix)
        cur = jnp.where(eidx == ix, -jnp.inf, cur)
    ex = [jnp.exp(v - vals[0]) for v in vals]
    den = ex[0] + ex[1] + ex[2] + ex[3]
    onehot = jnp.zeros(lg.shape, F32)
    for ix in idxs:
        onehot = onehot + (eidx == ix).astype(F32)
    ti = lax.broadcasted_iota(I32, (tm, tm), 0)
    tj = lax.broadcasted_iota(I32, (tm, tm), 1)
    tri = (ti <= tj).astype(BF16)
    cum = jnp.dot(onehot.astype(BF16), tri, preferred_element_type=F32)
    excl = cum - onehot + carry_ref[...]
    for k in range(TOP_K):
        ti_ref[k:k + 1, :] = idxs[k].astype(I32)
        tg_ref[k:k + 1, :] = ex[k] / den
        rk_ref[k:k + 1, :] = jnp.sum(jnp.where(eidx == idxs[k], excl, 0.0), axis=0, keepdims=True).astype(I32)
    carry_ref[...] = carry_ref[...] + jnp.sum(onehot, axis=1, keepdims=True)
    cnt_ref[...] = carry_ref[...]


def _router(x, g, mod3, sc_blk, sh_blk, rw_t, rb, n_prompt, dec_seq):
    n, d = x.shape
    n_exp = rw_t.shape[0]
    tm = _tile(np.gcd(n_prompt, dec_seq), 256)
    mrow = functools.partial(_mod_row, tm=tm, n_prompt=n_prompt, dec_seq=dec_seq)
    kspec = pl.BlockSpec((TOP_K, tm), lambda i: (0, i))
    return pl.pallas_call(
        _router_kernel, grid=(n // tm,),
        in_specs=[pl.BlockSpec((tm, d), lambda i: (i, 0)), pl.BlockSpec((1, d), lambda i: (0, 0)),
                  pl.BlockSpec((None, 1, d), lambda i: (mrow(i), 0, sc_blk)),
                  pl.BlockSpec((None, 1, d), lambda i: (mrow(i), 0, sh_blk)),
                  pl.BlockSpec((n_exp, d), lambda i: (0, 0)), pl.BlockSpec((n_exp, 1), lambda i: (0, 0))],
        out_specs=[pl.BlockSpec((tm, 1, d // 2), lambda i: (i, 0, 0)), kspec, kspec, kspec,
                   pl.BlockSpec((n_exp, 1), lambda i: (0, 0))],
        out_shape=[jax.ShapeDtypeStruct((n, 1, d // 2), jnp.uint32), jax.ShapeDtypeStruct((TOP_K, n), I32),
                   jax.ShapeDtypeStruct((TOP_K, n), F32), jax.ShapeDtypeStruct((TOP_K, n), I32),
                   jax.ShapeDtypeStruct((n_exp, 1), F32)],
        scratch_shapes=[pltpu.VMEM((n_exp, 1), F32)],
        compiler_params=_params(("arbitrary",)))(x, g.reshape(1, d), mod3, mod3, rw_t, rb.reshape(n_exp, 1))


def _start_row_gather(idx_ref, src_ref, buf_ref, sem, rows):
    def issue(r8, carry):
        for u in range(DMA_UNROLL):
            r = r8 * DMA_UNROLL + u
            pltpu.make_async_copy(src_ref.at[idx_ref[0, r]], buf_ref.at[r], sem).start()
        return carry

    lax.fori_loop(0, rows // DMA_UNROLL, issue, 0)


def _wait_row_gather(src_ref, buf_ref, sem, rows):
    pltpu.make_async_copy(src_ref.at[pl.ds(0, rows)], buf_ref, sem).wait()


def _ffn_up_kernel(be_ref, na_ref, idx0_ref, idx1_ref, hp_ref, wg_ref, wu_ref, bg_ref, bu_ref, a_ref,
                   xg0_ref, xg1_ref, x2d_ref, xbf_ref, sem):
    j, f = pl.program_id(0), pl.program_id(1)
    na = na_ref[0]
    bm, half = x2d_ref.shape

    @pl.when((f == 0) & (j < na))
    def _():
        @pl.when(j == 0)
        def _():
            _start_row_gather(idx0_ref, hp_ref, xg0_ref, sem.at[0], bm)

        for par in (0, 1):
            @pl.when(j % 2 == par)
            def _(par=par):
                cur, nxt = (xg0_ref, xg1_ref) if par == 0 else (xg1_ref, xg0_ref)
                _wait_row_gather(hp_ref, cur, sem.at[par], bm)

                @pl.when(j + 1 < na)
                def _():
                    _start_row_gather(idx1_ref, hp_ref, nxt, sem.at[1 - par], bm)

                x2d_ref[...] = cur[...].reshape(bm, half)
                bits = x2d_ref[...]
                xbf_ref[:, :half] = lax.bitcast_convert_type(bits & jnp.uint32(0xFFFF0000), F32).astype(BF16)
                xbf_ref[:, half:] = lax.bitcast_convert_type(bits << 16, F32).astype(BF16)

    @pl.when(j < na)
    def _():
        x = xbf_ref[...]
        g = jnp.dot(x, wg_ref[...], preferred_element_type=F32) + bg_ref[...]
        u = jnp.dot(x, wu_ref[...].astype(BF16), preferred_element_type=F32) + bu_ref[...]
        g = jnp.minimum(g, SWIGLU_LIMIT)
        u = jnp.clip(u, -SWIGLU_LIMIT, SWIGLU_LIMIT)
        a_ref[...] = ((u + 1.0) * g * jax.nn.sigmoid(SWIGLU_ALPHA * g)).astype(BF16)

    @pl.when(j >= na)
    def _():
        a_ref[...] = jnp.zeros_like(a_ref)


def _ffn_down_kernel(be_ref, na_ref, a_ref, wd_ref, bd_ref, o_ref):
    active = pl.program_id(0) < na_ref[0]

    @pl.when(active)
    def _():
        y = jnp.dot(a_ref[...], wd_ref[...], preferred_element_type=F32) + bd_ref[...]
        o_ref[...] = y.reshape(o_ref.shape)

    @pl.when(jnp.logical_not(active))
    def _():
        o_ref[...] = jnp.zeros_like(o_ref)


def _expert_ffn(hp, slot_tok, blk_expert, n_active, wg, wu, wd, bg, bu, bd, layer):
    n_slots = slot_tok.shape[0]
    d = 2 * hp.shape[2]
    n_exp, _, d_ff = wg.shape[1:]
    bm = MOE_ROWS
    nb = n_slots // bm
    tf = _tile(d_ff, 512)
    nf = d_ff // tf
    idx = slot_tok.reshape(nb, 1, bm)

    def jj(j, na):
        return jnp.minimum(j, na[0] - 1)

    def ff(j, f, na, last):
        return jnp.where(j < na[0], f, last)

    up_spec = pltpu.PrefetchScalarGridSpec(
        num_scalar_prefetch=2, grid=(nb, nf),
        in_specs=[pl.BlockSpec((None, 1, bm), lambda j, f, be, na: (jj(j, na), 0, 0), memory_space=pltpu.SMEM),
                  pl.BlockSpec((None, 1, bm), lambda j, f, be, na: (jnp.minimum(jj(j, na) + 1, nb - 1), 0, 0),
                               memory_space=pltpu.SMEM),
                  pl.BlockSpec(memory_space=pl.ANY),
                  pl.BlockSpec((None, None, d, tf), lambda j, f, be, na: (layer, be[jj(j, na)], 0, ff(j, f, na, nf - 1))),
                  pl.BlockSpec((None, None, d, tf), lambda j, f, be, na: (layer, be[jj(j, na)], 0, ff(j, f, na, nf - 1))),
                  pl.BlockSpec((None, None, 1, tf), lambda j, f, be, na: (layer, be[jj(j, na)], 0, ff(j, f, na, nf - 1))),
                  pl.BlockSpec((None, None, 1, tf), lambda j, f, be, na: (layer, be[jj(j, na)], 0, ff(j, f, na, nf - 1)))],
        out_specs=pl.BlockSpec((bm, tf), lambda j, f, be, na: (j, f)),
        scratch_shapes=[pltpu.VMEM((bm, 1, d // 2), jnp.uint32), pltpu.VMEM((bm, 1, d // 2), jnp.uint32),
                        pltpu.VMEM((bm, d // 2), jnp.uint32), pltpu.VMEM((bm, d), BF16),
                        pltpu.SemaphoreType.DMA((2,))])
    act = pl.pallas_call(
        _ffn_up_kernel, grid_spec=up_spec,
        out_shape=jax.ShapeDtypeStruct((n_slots, d_ff), BF16),
        compiler_params=_params(("arbitrary", "arbitrary")))(blk_expert, n_active, idx, idx, hp, wg, wu, bg, bu)

    tn = _tile(d, 2048)
    nn = d // tn
    down_spec = pltpu.PrefetchScalarGridSpec(
        num_scalar_prefetch=2, grid=(nb, nn),
        in_specs=[pl.BlockSpec((bm, d_ff), lambda j, c, be, na: (jj(j, na), 0)),
                  pl.BlockSpec((None, None, d_ff, tn), lambda j, c, be, na: (layer, be[jj(j, na)], 0, ff(j, c, na, nn - 1))),
                  pl.BlockSpec((None, None, 1, tn), lambda j, c, be, na: (layer, be[jj(j, na)], 0, ff(j, c, na, nn - 1)))],
        out_specs=pl.BlockSpec((bm, 1, tn), lambda j, c, be, na: (j, 0, c)))
    return pl.pallas_call(
        _ffn_down_kernel, grid_spec=down_spec,
        out_shape=jax.ShapeDtypeStruct((n_slots, 1, d), F32),
        compiler_params=_params(("arbitrary", "arbitrary")))(blk_expert, n_active, act, wd, bd)


def _combine_kernel(idx0_ref, idx1_ref, yb_ref, x_ref, g_ref, w_ref, o_ref, g0_ref, g1_ref, y2d_ref, sem):
    i, n = pl.program_id(0), pl.num_programs(0)
    rows, d = y2d_ref.shape
    tm = rows // TOP_K

    @pl.when(i == 0)
    def _():
        _start_row_gather(idx0_ref, yb_ref, g0_ref, sem.at[0], rows)

    for par in (0, 1):
        @pl.when(i % 2 == par)
        def _(par=par):
            cur, nxt = (g0_ref, g1_ref) if par == 0 else (g1_ref, g0_ref)
            _wait_row_gather(yb_ref, cur, sem.at[par], rows)

            @pl.when(i + 1 < n)
            def _():
                _start_row_gather(idx1_ref, yb_ref, nxt, sem.at[1 - par], rows)

            y2d_ref[...] = cur[...].reshape(rows, d)
            w = w_ref[...]
            y = w[:, 0:1] * y2d_ref[0:tm, :]
            for k in range(1, TOP_K):
                y = y + w[:, k:k + 1] * y2d_ref[k * tm:(k + 1) * tm, :]
            o_ref[...] = x_ref[...] + g_ref[...] * y


def _combine(x, yb, dest, gate, mod3, g_blk, n_prompt, dec_seq):
    n, d = x.shape
    tm = _tile(np.gcd(n_prompt, dec_seq), 128)
    nt = n // tm
    rows = TOP_K * tm
    idx = dest.reshape(TOP_K, nt, tm).transpose(1, 0, 2).reshape(nt, 1, rows)
    mrow = functools.partial(_mod_row, tm=tm, n_prompt=n_prompt, dec_seq=dec_seq)
    return pl.pallas_call(
        _combine_kernel, grid=(nt,),
        in_specs=[pl.BlockSpec((None, 1, rows), lambda i: (i, 0, 0), memory_space=pltpu.SMEM),
                  pl.BlockSpec((None, 1, rows), lambda i: (jnp.minimum(i + 1, nt - 1), 0, 0), memory_space=pltpu.SMEM),
                  pl.BlockSpec(memory_space=pl.ANY),
                  pl.BlockSpec((tm, d), lambda i: (i, 0)),
                  pl.BlockSpec((None, 1, d), lambda i: (mrow(i), 0, g_blk)),
                  pl.BlockSpec((tm, TOP_K), lambda i: (i, 0))],
        out_specs=pl.BlockSpec((tm, d), lambda i: (i, 0)),
        out_shape=jax.ShapeDtypeStruct((n, d), F32),
        scratch_shapes=[pltpu.VMEM((rows, 1, d), F32), pltpu.VMEM((rows, 1, d), F32), pltpu.VMEM((rows, d), F32),
                        pltpu.SemaphoreType.DMA((2,))],
        compiler_params=_params(("arbitrary",)))(idx, idx, yb, x, mod3, gate.T)


def _moe(x, p, layer, mod3, n_prompt, dec_seq):
    n, d = x.shape
    n_exp = p["router_w"].shape[-1]
    bm = MOE_ROWS
    h, topi, gate, rank, cnt = _router(x, p["norm_ffn"][layer], mod3, 4, 3, p["router_w"][layer].T,
                                       p["router_b"][layer], n_prompt, dec_seq)
    counts = cnt[:, 0].astype(I32)
    padded = (counts + bm - 1) // bm * bm
    pad_end = jnp.cumsum(padded)
    pad_start = pad_end - padded
    first = jnp.sum(jnp.where(topi[None] == jnp.arange(n_exp, dtype=I32)[:, None, None], pad_start[:, None, None], 0),
                    axis=0)
    dest = first + rank
    n_slots = n * TOP_K + n_exp * bm
    nb = n_slots // bm
    slot_tok = jnp.zeros((n_slots,), I32).at[dest.reshape(-1)].set(jnp.tile(jnp.arange(n, dtype=I32), TOP_K))
    blk_start = jnp.arange(nb, dtype=I32) * bm
    blk_expert = jnp.minimum(jnp.sum((pad_end[None, :] <= blk_start[:, None]).astype(I32), axis=1), n_exp - 1)
    n_active = (pad_end[-1:] // bm).astype(I32)
    yb = _expert_ffn(h, slot_tok, blk_expert, n_active, p["moe_w_gate"], p["moe_w_up"], p["moe_w_down"],
                     p["moe_b_gate"], p["moe_b_up"], p["moe_b_down"], layer)
    return _combine(x, yb, dest, gate, mod3, 5, n_prompt, dec_seq)


def kernel(x_prompt, x_sample, cache_mla_ckv, cache_mla_krope, cache_gqa_k, cache_gqa_v, state_mlstm_C, state_mlstm_n, state_mlstm_m, c, c_ctx, w_mod, b_mod, norm_attn, norm_ffn, w_in, mla_q_norm, mla_w_qb, mla_kv_norm, mla_w_kvb, gqa_q_norm, gqa_k_norm, mlstm_gate_bias, mlstm_out_norm, w_branch, w_out, router_w, router_b, moe_w_gate, moe_b_gate, moe_w_up, moe_b_up, moe_w_down, moe_b_down, final_norm):
    bp, sp, d = x_prompt.shape
    bs, ss, _ = x_sample.shape
    depth = w_mod.shape[0]
    past = cache_mla_ckv.shape[2]
    n_p, n_s = bp * sp, bs * ss
    n = n_p + n_s
    tk_s = ss + past
    H = MLSTM_HEADS
    lay, _ = _proj_layout(d)
    n_exp = router_w.shape[-1]

    x = jnp.concatenate([x_prompt.reshape(n_p, d), x_sample.reshape(n_s, d)], axis=0)
    cond = jnp.zeros((16, d), F32).at[0].set(c_ctx).at[1:1 + bs].set(c)
    tabs = _rope_tables(n_p, bs, ss, MLA_ROPE) + _rope_tables(n_p, bs, ss, GQA_HEAD_DIM)
    moe = dict(router_w=router_w, router_b=router_b, norm_ffn=norm_ffn,
               moe_w_gate=moe_w_gate.astype(BF16), moe_w_up=moe_w_up, moe_w_down=moe_w_down.astype(BF16),
               moe_b_gate=moe_b_gate.reshape(depth, n_exp, 1, -1), moe_b_up=moe_b_up.reshape(depth, n_exp, 1, -1),
               moe_b_down=moe_b_down.reshape(depth, n_exp, 1, d))
    zc = jnp.zeros((bp, 2, H, MLSTM_DK, MLSTM_DV), F32)
    zn = jnp.zeros((bp, 2, H, 1, MLSTM_DK), F32)
    zm = jnp.zeros((bp, 2, H, 1, 1), F32)
    ctx = []
    for l in range(depth):
        mod3 = _modulation(cond, w_mod[l], b_mod[l]).reshape(16, 1, 6 * d)
        h = _norm_mod(x, norm_attn[l], mod3, 1, 0, n_p, ss)
        proj = _matmul(h, _build_w_in(w_in[l], d), F32, 1024, 512)
        qn = jnp.pad(mla_q_norm[l], (0, Q_LORA_PAD - MLA_Q_LORA)).reshape(1, Q_LORA_PAD)
        qm, ckv32, ckv16, kr16, qg, kg32, kg16, v16 = _prep(
            proj, lay, tabs, qn, mla_kv_norm[l].reshape(1, -1), gqa_q_norm[l].reshape(1, -1),
            gqa_k_norm[l].reshape(1, -1), _build_w_qb(mla_w_qb[l]))
        ckv_all = jnp.concatenate([
            jnp.concatenate([ckv16[n_p:].reshape(bs, ss, -1), cache_mla_ckv[:, l].astype(BF16)], axis=1).reshape(bs * tk_s, -1),
            ckv16[:n_p]], axis=0)
        kr_s = jnp.concatenate([kr16[n_p:].reshape(bs, ss, LANE),
                                jnp.pad(cache_mla_krope[:, l], ((0, 0), (0, 0), (0, LANE - MLA_ROPE))).astype(BF16)],
                               axis=1).reshape(bs * tk_s, LANE)
        km, vm = _kv_expand(ckv_all, jnp.concatenate([kr_s, kr16[:n_p]], axis=0), _build_w_kvb(mla_w_kvb[l]))
        kg_s = jnp.concatenate([kg16[n_p:].reshape(bs, ss, -1), cache_gqa_k[:, l].reshape(bs, past, -1).astype(BF16)],
                               axis=1).reshape(bs * tk_s, -1)
        vg_s = jnp.concatenate([v16[n_p:].reshape(bs, ss, -1), cache_gqa_v[:, l].reshape(bs, past, -1).astype(BF16)],
                               axis=1).reshape(bs * tk_s, -1)
        ya = jnp.zeros((n, MLA_WIDTH), BF16)
        mla = dict(heads=MLA_HEADS, dq=MLA_QK, shared_kv=False)
        ya = _attention_call(ya, qm, km, vm, batch=bp, tq_len=sp, tk_len=sp, hp=MLA_HEADS, tq_cap=256, q_row0=0,
                             k_row0=bs * tk_s, **mla)
        ya = _attention_call(ya, qm, km, vm, batch=bs, tq_len=ss, tk_len=tk_s, hp=2, tq_cap=256, q_row0=n_p,
                             k_row0=0, **mla)
        gqa = dict(heads=GQA_HEADS, hp=GQA_HEADS // GQA_KV_HEADS, dq=GQA_HEAD_DIM, tq_cap=256, shared_kv=True)
        yb = jnp.zeros((n, GQA_WIDTH), BF16)
        yb = _attention_call(yb, qg, kg16, v16, batch=bp, tq_len=sp, tk_len=sp, q_row0=0, k_row0=0, **gqa)
        yb = _attention_call(yb, qg, kg_s, vg_s, batch=bs, tq_len=ss, tk_len=tk_s, q_row0=n_p, k_row0=0, **gqa)
        mg0 = lay["mg"][0]
        mg = proj[:, mg0:mg0 + 4 * H].reshape(n // MLSTM_CHUNK, MLSTM_CHUNK, 4, H)
        gr = mg.transpose(3, 0, 2, 1)
        gc = mg.transpose(3, 0, 1, 2)
        bias = mlstm_gate_bias[l].reshape(-1)
        gnorm = mlstm_out_norm[l].reshape(H, 1, MLSTM_DV)
        yc = jnp.zeros((n, MLSTM_WIDTH), BF16)
        yc, c_new, n_new, m_new = _mlstm(yc, proj, lay, gr, gc, bias, zc, zn, zm, gnorm, batch=bp, seq=sp, row0=0)
        yc, _, _, _ = _mlstm(yc, proj, lay, gr, gc, bias, state_mlstm_C[:, l],
                             state_mlstm_n[:, l].reshape(bs, 2, H, 1, MLSTM_DK),
                             state_mlstm_m[:, l].reshape(bs, 2, H, 1, 1), gnorm, batch=bs, seq=ss, row0=n_p)
        wb = w_branch[l].astype(BF16)
        wbr = jnp.concatenate([wb[MLA_WIDTH:MLA_WIDTH + GQA_WIDTH], wb[:MLA_WIDTH], wb[MLA_WIDTH + GQA_WIDTH:]], axis=0)
        z = _merge(ya, yb, yc, proj, lay, wbr, d)
        x = _matmul_residual(z, w_out[l].astype(BF16), x, mod3, 2, n_p, ss)
        x = _moe(x, moe, l, mod3, n_p, ss)
        kr0, gv0 = lay["kr"][0], lay["gv"][0]
        ctx.append((ckv32[:n_p].reshape(bp, sp, -1), proj[:n_p, kr0:kr0 + MLA_ROPE].reshape(bp, sp, MLA_ROPE),
                    kg32[:n_p].reshape(bp, sp, GQA_KV_HEADS, GQA_HEAD_DIM),
                    proj[:n_p, gv0:gv0 + GQA_KV_WIDTH].reshape(bp, sp, GQA_KV_HEADS, GQA_HEAD_DIM),
                    c_new, n_new.reshape(bp, 2, H, MLSTM_DK), m_new.reshape(bp, 2, H)))
    y_p = _rmsnorm(x, final_norm, 0, n_p)
    y_s = _rmsnorm(x, final_norm, n_p, n_s)
    outs = [jnp.stack([t[i] for t in ctx], axis=1) for i in range(7)]
    return (y_p.reshape(bp, sp, d), y_s.reshape(bs, ss, d), *outs)
```
